```python
import jax, jax.numpy as jnp
from jax import lax
import numpy as np

D_MODEL = 1024
BATCH = 8
SEQ = 2048
DEPTH = 4
DEC_BATCH = 128
DEC_SEQ = 4
PAST_LEN = 16384
PAGE_SIZE = 128

N_META = 16
HGRN_HEADS = 8
HGRN_DK = 128
HGRN_DV = D_MODEL // HGRN_HEADS
QK_W = HGRN_HEADS * HGRN_DK
V_W = HGRN_HEADS * HGRN_DV
CHUNK = 64
PAD_FRONT = CHUNK - N_META
CONV_K = 3
CONV_W = 1024
D_FF = 2816
IN_COLS = 2 * QK_W + 2 * V_W + 3 * CONV_W + 2 * D_MODEL
ALPHA = (2 * DEPTH) ** 0.25
BETA = (8 * DEPTH) ** -0.25
LN_EPS = 1e-5
RMS_EPS = 1e-6

kernel_name = "hgrn2_shortconv_gated_hybrid_step"


def _layernorm(x, g, b):
    xf = x.astype(jnp.float32)
    mu = jnp.mean(xf, axis=-1, keepdims=True)
    var = jnp.mean(jnp.square(xf - mu), axis=-1, keepdims=True)
    y = (xf - mu) * lax.rsqrt(var + LN_EPS) * g.astype(jnp.float32) + b.astype(jnp.float32)
    return y.astype(x.dtype)


def _swiglu(x, w_gu, w_down):
    gu = jnp.einsum('btd,df->btf', x, w_gu)
    g, u = jnp.split(gu, 2, axis=-1)
    return jnp.einsum('btf,fd->btd', jax.nn.silu(g) * u, w_down)


def _gla_chunk(S, qkvg):
    q, k, v, g = qkvg
    C = q.shape[2]
    b = jnp.cumsum(g, axis=2)
    causal = jnp.tril(jnp.ones((C, C), dtype=bool))
    diff = b[:, :, :, None, :] - b[:, :, None, :, :]
    decay = jnp.exp(jnp.where(causal[None, None, :, :, None], diff, -jnp.inf))
    scores = jnp.einsum('bhtd,bhsd,bhtsd->bhts', q, k, decay)
    o = jnp.einsum('bhts,bhsv->bhtv', scores, v) + jnp.einsum('bhtd,bhdv->bhtv', q * jnp.exp(b), S)
    b_last = b[:, :, -1:, :]
    S_new = jnp.exp(b_last[:, :, 0, :, None]) * S + jnp.einsum('bhsd,bhsv->bhdv', k * jnp.exp(b_last - b), v)
    return S_new, o


def _hgrn_recurrence(q, k, v, log_f, S0, chunk):
    Bn, H, T, _ = q.shape
    nc = T // chunk

    def split(a):
        return jnp.moveaxis(a.reshape(Bn, H, nc, chunk, a.shape[-1]), 2, 0)

    S, o = lax.scan(_gla_chunk, S0, (split(q), split(k), split(v), split(log_f)))
    o = jnp.moveaxis(o, 0, 2).reshape(Bn, H, T, HGRN_DV)
    return o, S


def _heads(a, d, pad):
    a = a.reshape(a.shape[0], a.shape[1], HGRN_HEADS, d)
    a = jnp.pad(a, ((0, 0), (pad, 0), (0, 0), (0, 0)))
    return a.transpose(0, 2, 1, 3)


def _token_mixer(x, S0, conv_buf, lb, pad, chunk, w_in, norm_w, conv_w, w_a, w_b, w_o):
    Bn, T, _ = x.shape
    z = jnp.einsum('btd,dc->btc', x, w_in)
    idx = [int(i) for i in np.cumsum([QK_W, QK_W, V_W, V_W, CONV_W, CONV_W, CONV_W, D_MODEL])]
    q, f_pre, v, og, b_gate, c_gate, h, gate_a, gate_b = jnp.split(z, idx, axis=-1)

    lb = lb.astype(jnp.float32)
    log_f = jnp.logaddexp(jnp.log(lb), jnp.log1p(-lb) + jax.nn.log_sigmoid(f_pre.astype(jnp.float32)))
    k = -jnp.expm1(log_f)
    q = jax.nn.silu(q.astype(jnp.float32))
    o, S_new = _hgrn_recurrence(_heads(q, HGRN_DK, pad), _heads(k, HGRN_DK, pad),
                                _heads(v.astype(jnp.float32), HGRN_DV, pad), _heads(log_f, HGRN_DK, pad),
                                S0.astype(jnp.float32), chunk)
    o = o[:, :, pad:].transpose(0, 2, 1, 3)
    o = o * lax.rsqrt(jnp.mean(jnp.square(o), axis=-1, keepdims=True) + RMS_EPS) * norm_w.astype(jnp.float32)
    og = og.reshape(Bn, T, HGRN_HEADS, HGRN_DV).astype(jnp.float32)
    o = (o * jax.nn.silu(og)).reshape(Bn, T, V_W).astype(x.dtype)
    y_a = jnp.einsum('btc,cd->btd', o, w_a)

    u = c_gate * h
    uu = jnp.concatenate([conv_buf.astype(u.dtype), u], axis=1)
    conv = conv_w[0] * uu[:, 0:T]
    for j in range(1, CONV_K):
        conv = conv + conv_w[j] * uu[:, j:j + T]
    new_buf = uu[:, T:]
    y_b = jnp.einsum('btc,cd->btd', b_gate * conv, w_b)

    m = jax.nn.sigmoid(gate_a) * y_a + jax.nn.sigmoid(gate_b) * y_b
    out = jnp.einsum('btd,de->bte', m, w_o)
    return out, S_new.astype(x.dtype), new_buf


def _trunk(x, S_init, buf_init, pad, chunk, lb_all, p):
    new_S, new_buf = [], []
    for l in range(DEPTH):
        x = _layernorm(ALPHA * x + 0.5 * _swiglu(x, p['ffn1_w_gu'][l], p['ffn1_w_down'][l]),
                       p['ln_g'][l, 0], p['ln_b'][l, 0])
        m, S, buf = _token_mixer(x, S_init[l], buf_init[l], lb_all[l], pad, chunk,
                                 p['w_in'][l], p['hgrn_norm_w'][l], p['conv_w'][l],
                                 p['w_branch_a'][l], p['w_branch_b'][l], p['w_out'][l])
        x = _layernorm(ALPHA * x + m, p['ln_g'][l, 1], p['ln_b'][l, 1])
        x = _layernorm(ALPHA * x + 0.5 * _swiglu(x, p['ffn2_w_gu'][l], p['ffn2_w_down'][l]),
                       p['ln_g'][l, 2], p['ln_b'][l, 2])
        new_S.append(S)
        new_buf.append(buf)
    return x, jnp.stack(new_S), jnp.stack(new_buf)


def setup_inputs(seed: int = 0) -> dict:
    key = jax.random.key(seed)
    ks = jax.random.split(key, 20)
    nrm = jax.random.normal
    f32 = jnp.float32
    return {
        "x_prompt": nrm(ks[0], (BATCH, SEQ, D_MODEL), f32),
        "x_sample": nrm(ks[1], (DEC_BATCH, DEC_SEQ, D_MODEL), f32),
        "state_hgrn": 0.3 * nrm(ks[2], (DEPTH, DEC_BATCH, HGRN_HEADS, HGRN_DK, HGRN_DV), f32),
        "state_conv": nrm(ks[3], (DEPTH, DEC_BATCH, CONV_K - 1, CONV_W), f32),
        "meta_tokens": nrm(ks[4], (N_META, D_MODEL), f32),
        "w_in": nrm(ks[5], (DEPTH, D_MODEL, IN_COLS), f32) * D_MODEL ** -0.5,
        "hgrn_lb_logits": 0.1 * nrm(ks[6], (DEPTH, QK_W), f32),
        "hgrn_norm_w": 1.0 + 0.02 * nrm(ks[7], (DEPTH, HGRN_DV), f32),
        "conv_w": nrm(ks[8], (DEPTH, CONV_K, CONV_W), f32) * CONV_K ** -0.5,
        "w_branch_a": nrm(ks[9], (DEPTH, V_W, D_MODEL), f32) * (V_W ** -0.5 * BETA),
        "w_branch_b": nrm(ks[10], (DEPTH, CONV_W, D_MODEL), f32) * (CONV_W ** -0.5 * BETA),
        "w_out": nrm(ks[11], (DEPTH, D_MODEL, D_MODEL), f32) * (D_MODEL ** -0.5 * BETA),
        "ffn1_w_gu": nrm(ks[12], (DEPTH, D_MODEL, 2 * D_FF), f32) * D_MODEL ** -0.5,
        "ffn1_w_down": nrm(ks[13], (DEPTH, D_FF, D_MODEL), f32) * (D_FF ** -0.5 * BETA),
        "ffn2_w_gu": nrm(ks[14], (DEPTH, D_MODEL, 2 * D_FF), f32) * D_MODEL ** -0.5,
        "ffn2_w_down": nrm(ks[15], (DEPTH, D_FF, D_MODEL), f32) * (D_FF ** -0.5 * BETA),
        "ln_g": 1.0 + 0.02 * nrm(ks[16], (DEPTH, 3, D_MODEL), f32),
        "ln_b": 0.02 * nrm(ks[17], (DEPTH, 3, D_MODEL), f32),
    }


def reference(x_prompt, x_sample, state_hgrn, state_conv, meta_tokens, w_in, hgrn_lb_logits,
              hgrn_norm_w, conv_w, w_branch_a, w_branch_b, w_out, ffn1_w_gu, ffn1_w_down,
              ffn2_w_gu, ffn2_w_down, ln_g, ln_b):
    p = dict(w_in=w_in, hgrn_norm_w=hgrn_norm_w, conv_w=conv_w, w_branch_a=w_branch_a,
             w_branch_b=w_branch_b, w_out=w_out, ffn1_w_gu=ffn1_w_gu, ffn1_w_down=ffn1_w_down,
             ffn2_w_gu=ffn2_w_gu, ffn2_w_down=ffn2_w_down, ln_g=ln_g, ln_b=ln_b)
    lb_all = jnp.cumsum(jax.nn.softmax(hgrn_lb_logits.astype(jnp.float32), axis=0), axis=0)
    lb_all = lb_all - lb_all[0:1]

    bp = x_prompt.shape[0]
    meta = jnp.broadcast_to(meta_tokens[None].astype(x_prompt.dtype), (bp, N_META, D_MODEL))
    xp = jnp.concatenate([meta, x_prompt], axis=1)
    S0p = jnp.zeros((DEPTH, bp, HGRN_HEADS, HGRN_DK, HGRN_DV), x_prompt.dtype)
    buf0p = jnp.zeros((DEPTH, bp, CONV_K - 1, CONV_W), x_prompt.dtype)
    yp, new_hgrn_prompt, new_conv_prompt = _trunk(xp, S0p, buf0p, PAD_FRONT, CHUNK, lb_all, p)
    y_prompt = yp[:, N_META:]

    y_sample, new_hgrn_sample, new_conv_sample = _trunk(x_sample, state_hgrn, state_conv, 0,
                                                        x_sample.shape[1], lb_all, p)
    return (y_prompt, y_sample, new_hgrn_prompt, new_conv_prompt, new_hgrn_sample, new_conv_sample)
```

```python
import functools

import jax
import jax.numpy as jnp
from jax import lax
from jax.experimental import pallas as pl
from jax.experimental.pallas import tpu as pltpu

D_MODEL = 1024
DEPTH = 4
N_META = 16
HEADS = 8
DK = 128
DV = 128
CONV_K = 3
D_FF = 2816
N_SPLITS = 9
ALPHA = (2 * DEPTH) ** 0.25
LN_EPS = 1e-5
RMS_EPS = 1e-6

LANES = 128
SUBLANES = 8
VMEM_LIMIT_BYTES = 56 * 1024 * 1024

F32 = jnp.float32
BF16 = jnp.bfloat16

_NT = (((1,), (1,)), ((), ()))
_TN = (((0,), (0,)), ((), ()))


def _const_spec(shape):
    nd = len(shape)
    return pl.BlockSpec(shape, lambda *_: (0,) * nd, pipeline_mode=pl.Buffered(1))


def _layernorm(y, g, b):
    mu = jnp.mean(y, axis=-1, keepdims=True)
    d = y - mu
    var = jnp.mean(d * d, axis=-1, keepdims=True)
    return d * lax.rsqrt(var + LN_EPS) * g + b


def _sigmoid(x):
    return 1.0 / (1.0 + jnp.exp(-x))


def _silu(x):
    return x * _sigmoid(x)


def _lb_kernel(logits_ref, loglb_ref, log1m_ref, om_ref):
    x = logits_ref[...]
    rows = [x[l:l + 1, :] for l in range(DEPTH)]
    m = functools.reduce(jnp.maximum, rows)
    e = [jnp.exp(r - m) for r in rows]
    s = functools.reduce(lambda a, c: a + c, e)
    cs = []
    acc = None
    for l in range(DEPTH):
        p = e[l] / s
        acc = p if acc is None else acc + p
        cs.append(acc)
    for l in range(DEPTH):
        lb = cs[l] - cs[0]
        loglb_ref[l:l + 1, :] = jnp.log(lb)
        log1m_ref[l:l + 1, :] = jnp.log1p(-lb)
        om_ref[l:l + 1, :] = 1.0 - lb


def _lower_bounds(logits):
    shp = jax.ShapeDtypeStruct(logits.shape, F32)
    return pl.pallas_call(_lb_kernel, out_shape=(shp, shp, shp), name="lb_prep")(logits)


def _ffn_ln_kernel(x_ref, wgu_ref, wd_ref, g_ref, b_ref, o_ref):
    x = x_ref[...]
    gu = jnp.dot(x.astype(BF16), wgu_ref[...], preferred_element_type=F32)
    h = _silu(gu[:, :D_FF]) * gu[:, D_FF:]
    y = jnp.dot(h.astype(BF16), wd_ref[...], preferred_element_type=F32)
    o_ref[...] = _layernorm(ALPHA * x + 0.5 * y, g_ref[...], b_ref[...])


def _ffn_ln(x, wgu, wd, g, b, tm):
    n = x.shape[0]
    row = pl.BlockSpec((tm, D_MODEL), lambda i: (i, 0))
    return pl.pallas_call(
        _ffn_ln_kernel,
        out_shape=jax.ShapeDtypeStruct((n, D_MODEL), F32),
        grid=(n // tm,),
        in_specs=[row, _const_spec(wgu.shape), _const_spec(wd.shape),
                  _const_spec(g.shape), _const_spec(b.shape)],
        out_specs=row,
        compiler_params=pltpu.CompilerParams(
            dimension_semantics=("parallel",), vmem_limit_bytes=VMEM_LIMIT_BYTES),
        name="ffn_ln",
    )(x, wgu, wd, g, b)


def _inproj_kernel(x_ref, w_ref, loglb_ref, log1m_ref, om_ref,
                   qs_ref, bc_ref, kk_ref, v_ref, sog_ref, bg_ref, u_ref, sa_ref, sb_ref,
                   *, tm, chunk):
    xb = x_ref[...].astype(BF16)

    def z(j):
        return jnp.dot(xb, w_ref[:, j * D_MODEL:(j + 1) * D_MODEL], preferred_element_type=F32)

    qs_ref[...] = _silu(z(0))

    fp = z(1)
    e = jnp.exp(-jnp.abs(fp))
    log_sig = jnp.minimum(fp, 0.0) - jnp.log1p(e)
    a = loglb_ref[...]
    c = log1m_ref[...] + log_sig
    log_f = jnp.maximum(a, c) + jnp.log1p(jnp.exp(-jnp.abs(a - c)))
    kk_ref[...] = om_ref[...] * (jnp.where(fp >= 0.0, e, 1.0) / (1.0 + e))

    sub = min(tm, LANES)
    r = lax.broadcasted_iota(jnp.int32, (sub, sub), 0)
    cidx = lax.broadcasted_iota(jnp.int32, (sub, sub), 1)
    tri = jnp.where((cidx <= r) & (r // chunk == cidx // chunk), 1.0, 0.0).astype(BF16)
    for s in range(tm // sub):
        g = log_f[s * sub:(s + 1) * sub, :]
        g_hi = g.astype(BF16)
        g_lo = (g - g_hi.astype(F32)).astype(BF16)
        cum = jnp.dot(tri, jnp.concatenate([g_hi, g_lo], axis=1), preferred_element_type=F32)
        bc_ref[s * sub:(s + 1) * sub, :] = cum[:, :D_MODEL] + cum[:, D_MODEL:]

    v_ref[...] = z(2)
    sog_ref[...] = _silu(z(3))
    bg_ref[...] = z(4)
    u_ref[...] = z(5) * z(6)
    sa_ref[...] = _sigmoid(z(7))
    sb_ref[...] = _sigmoid(z(8))


def _inproj(x, w_in, loglb, log1m, om, tm, chunk):
    n = x.shape[0]
    row = pl.BlockSpec((tm, D_MODEL), lambda i: (i, 0))
    vec = _const_spec((1, D_MODEL))
    shp = jax.ShapeDtypeStruct((n, D_MODEL), F32)
    return pl.pallas_call(
        functools.partial(_inproj_kernel, tm=tm, chunk=chunk),
        out_shape=(shp,) * N_SPLITS,
        grid=(n // tm,),
        in_specs=[row, _const_spec(w_in.shape), vec, vec, vec],
        out_specs=(row,) * N_SPLITS,
        compiler_params=pltpu.CompilerParams(
            dimension_semantics=("parallel",), vmem_limit_bytes=VMEM_LIMIT_BYTES),
        name="inproj",
    )(x, w_in, loglb, log1m, om)


def _hgrn_chunk(q, k, v, b, st, chunk):
    bl = b[chunk - 1:chunk, :]
    qd = q * jnp.exp(b)
    o = lax.dot_general(qd.astype(BF16), st.astype(BF16), _NT, preferred_element_type=F32)
    kd = k * jnp.exp(bl - b)
    upd = lax.dot_general(v.astype(BF16), kd.astype(BF16), _TN, preferred_element_type=F32)
    st_new = st * jnp.exp(bl) + upd

    if chunk >= 2 * SUBLANES:
        row = lax.broadcasted_iota(jnp.int32, (chunk, chunk), 0)
        col = lax.broadcasted_iota(jnp.int32, (chunk, chunk), 1)
        scores = jnp.zeros((chunk, chunk), F32)
        half = chunk // 2
        while half >= SUBLANES:
            nb = chunk // (2 * half)
            b3 = b.reshape(nb, 2 * half, LANES)
            upper = lax.broadcasted_iota(jnp.int32, (1, 2 * half, 1), 1) >= half
            dec = jnp.exp(-jnp.abs(b3 - b3[:, half - 1:half, :]))
            mixed = jnp.where(upper, q.reshape(nb, 2 * half, LANES), k.reshape(nb, 2 * half, LANES)) * dec
            qh = jnp.where(upper, mixed, 0.0).reshape(chunk, LANES)
            kh = jnp.where(upper, 0.0, mixed).reshape(chunk, LANES)
            part = lax.dot_general(qh.astype(BF16), kh.astype(BF16), _NT, preferred_element_type=F32)
            scores = scores + jnp.where(row // (2 * half) == col // (2 * half), part, 0.0)
            half //= 2
        o = o + jnp.dot(scores.astype(BF16), v.astype(BF16), preferred_element_type=F32)

    cb = min(SUBLANES, chunk)
    nb = chunk // cb
    q3 = q.reshape(nb, cb, LANES)
    k3 = k.reshape(nb, cb, LANES)
    v3 = v.reshape(nb, cb, LANES)
    b3 = b.reshape(nb, cb, LANES)
    t_idx = lax.broadcasted_iota(jnp.int32, (1, cb, 1), 1)
    od = jnp.zeros((nb, cb, LANES), F32)
    for s in range(cb):
        dec = jnp.exp(jnp.minimum(b3 - b3[:, s:s + 1, :], 0.0))
        a = jnp.sum(q3 * k3[:, s:s + 1, :] * dec, axis=-1, keepdims=True)
        od = od + jnp.where(t_idx >= s, a, 0.0) * v3[:, s:s + 1, :]
    return o + od.reshape(chunk, LANES), st_new


def _hgrn_kernel(qs_ref, bc_ref, kk_ref, v_ref, sog_ref, s0_ref, nw_ref, o_ref, sn_ref, st_ref,
                 *, sb, tb, chunk):
    j = pl.program_id(1)

    @pl.when(j == 0)
    def _():
        for s in range(sb):
            for h in range(HEADS):
                st_ref[s, h] = s0_ref[s, h].T

    nw = nw_ref[...]

    def head_body(idx, carry):
        s = idx // ((tb // chunk) * HEADS)
        rem = idx % ((tb // chunk) * HEADS)
        c = rem // HEADS
        h = rem % HEADS
        rows = pl.ds(pl.multiple_of(c * chunk, chunk), chunk)
        lanes = pl.ds(pl.multiple_of(h * LANES, LANES), LANES)
        o, st_new = _hgrn_chunk(qs_ref[s, rows, lanes], kk_ref[s, rows, lanes], v_ref[s, rows, lanes],
                                bc_ref[s, rows, lanes], st_ref[s, h], chunk)
        st_ref[s, h] = st_new
        o = o * lax.rsqrt(jnp.mean(o * o, axis=-1, keepdims=True) + RMS_EPS) * nw
        o_ref[s, rows, lanes] = o * sog_ref[s, rows, lanes]
        return carry

    lax.fori_loop(0, sb * (tb // chunk) * HEADS, head_body, 0)

    @pl.when(j == pl.num_programs(1) - 1)
    def _():
        for s in range(sb):
            for h in range(HEADS):
                sn_ref[s, h] = st_ref[s, h].T


def _hgrn(qs, bc, kk, v, sog, s0, nw, *, seq_len, chunk, sb, tb, shared_s0):
    n = qs.shape[0]
    nseq = n // seq_len
    r3 = lambda a: a.reshape(nseq, seq_len, D_MODEL)
    row = pl.BlockSpec((sb, tb, D_MODEL), lambda i, j: (i, j, 0))
    if shared_s0:
        s0_spec = pl.BlockSpec((sb, HEADS, DK, DV), lambda i, j: (0, 0, 0, 0))
    else:
        s0_spec = pl.BlockSpec((sb, HEADS, DK, DV), lambda i, j: (i, 0, 0, 0))
    o, s_new = pl.pallas_call(
        functools.partial(_hgrn_kernel, sb=sb, tb=tb, chunk=chunk),
        out_shape=(jax.ShapeDtypeStruct((nseq, seq_len, D_MODEL), F32),
                   jax.ShapeDtypeStruct((nseq, HEADS, DK, DV), F32)),
        grid=(nseq // sb, seq_len // tb),
        in_specs=[row] * 5 + [s0_spec, _const_spec((1, DV))],
        out_specs=(row, pl.BlockSpec((sb, HEADS, DK, DV), lambda i, j: (i, 0, 0, 0))),
        scratch_shapes=[pltpu.VMEM((sb, HEADS, DV, DK), F32)],
        compiler_params=pltpu.CompilerParams(
            dimension_semantics=("parallel", "arbitrary"), vmem_limit_bytes=VMEM_LIMIT_BYTES),
        name="hgrn",
    )(r3(qs), r3(bc), r3(kk), r3(v), r3(sog), s0, nw)
    return o.reshape(n, D_MODEL), s_new


def _mixer_out_kernel(o_ref, um2_ref, um1_ref, u_ref, bg_ref, sa_ref, sb_ref, x_ref,
                      cw_ref, wa_ref, wb_ref, wo_ref, g_ref, b_ref, out_ref):
    conv = cw_ref[0:1, :] * um2_ref[...] + cw_ref[1:2, :] * um1_ref[...] + cw_ref[2:3, :] * u_ref[...]
    y_b = jnp.dot((bg_ref[...] * conv).astype(BF16), wb_ref[...], preferred_element_type=F32)
    y_a = jnp.dot(o_ref[...].astype(BF16), wa_ref[...], preferred_element_type=F32)
    m = sa_ref[...] * y_a + sb_ref[...] * y_b
    out = jnp.dot(m.astype(BF16), wo_ref[...], preferred_element_type=F32)
    out_ref[...] = _layernorm(ALPHA * x_ref[...] + out, g_ref[...], b_ref[...])


def _mixer_out(o, um2, um1, u, bg, sa, sb, x, cw, wa, wb, wo, g, b, tm):
    n = x.shape[0]
    row = pl.BlockSpec((tm, D_MODEL), lambda i: (i, 0))
    wspec = _const_spec((D_MODEL, D_MODEL))
    vec = _const_spec((1, D_MODEL))
    return pl.pallas_call(
        _mixer_out_kernel,
        out_shape=jax.ShapeDtypeStruct((n, D_MODEL), F32),
        grid=(n // tm,),
        in_specs=[row] * 8 + [_const_spec((CONV_K, D_MODEL)), wspec, wspec, wspec, vec, vec],
        out_specs=row,
        compiler_params=pltpu.CompilerParams(
            dimension_semantics=("parallel",), vmem_limit_bytes=VMEM_LIMIT_BYTES),
        name="mixer_out",
    )(o, um2, um1, u, bg, sa, sb, x, cw, wa, wb, wo, g, b)


def _layer(x, s0, buf0, p, *, seq_len, chunk, tm, sb, tb, shared_s0):
    n = x.shape[0]
    nseq = n // seq_len
    x = _ffn_ln(x, p["ffn1_wgu"], p["ffn1_wd"], p["ln_g"][0], p["ln_b"][0], tm)
    qs, bc, kk, v, sog, bg, u, sa, sbg = _inproj(x, p["w_in"], p["loglb"], p["log1m"], p["om"], tm, chunk)
    o, s_new = _hgrn(qs, bc, kk, v, sog, s0, p["norm_w"], seq_len=seq_len, chunk=chunk, sb=sb, tb=tb,
                     shared_s0=shared_s0)
    uu = jnp.concatenate([buf0, u.reshape(nseq, seq_len, D_MODEL)], axis=1)
    um2 = uu[:, 0:seq_len].reshape(n, D_MODEL)
    um1 = uu[:, 1:seq_len + 1].reshape(n, D_MODEL)
    new_buf = uu[:, seq_len:]
    x = _mixer_out(o, um2, um1, u, bg, sa, sbg, x, p["conv_w"], p["w_a"], p["w_b"], p["w_o"],
                   p["ln_g"][1], p["ln_b"][1], tm)
    x = _ffn_ln(x, p["ffn2_wgu"], p["ffn2_wd"], p["ln_g"][2], p["ln_b"][2], tm)
    return x, s_new, new_buf


def kernel(x_prompt, x_sample, state_hgrn, state_conv, meta_tokens, w_in, hgrn_lb_logits, hgrn_norm_w,
           conv_w, w_branch_a, w_branch_b, w_out, ffn1_w_gu, ffn1_w_down, ffn2_w_gu, ffn2_w_down,
           ln_g, ln_b):
    bp, seq, _ = x_prompt.shape
    bs, dec_seq, _ = x_sample.shape
    loglb, log1m, om = _lower_bounds(hgrn_lb_logits.astype(F32))

    def layer_params(l):
        return dict(
            w_in=w_in[l].astype(BF16), loglb=loglb[l:l + 1], log1m=log1m[l:l + 1], om=om[l:l + 1],
            norm_w=hgrn_norm_w[l][None, :], conv_w=conv_w[l],
            w_a=w_branch_a[l].astype(BF16), w_b=w_branch_b[l].astype(BF16), w_o=w_out[l].astype(BF16),
            ffn1_wgu=ffn1_w_gu[l].astype(BF16), ffn1_wd=ffn1_w_down[l].astype(BF16),
            ffn2_wgu=ffn2_w_gu[l].astype(BF16), ffn2_wd=ffn2_w_down[l].astype(BF16),
            ln_g=[ln_g[l, i][None, :] for i in range(3)], ln_b=[ln_b[l, i][None, :] for i in range(3)])

    xm = meta_tokens.astype(F32)
    xp = x_prompt.reshape(bp * seq, D_MODEL)
    xs = x_sample.reshape(bs * dec_seq, D_MODEL)
    hp, cp, hs, cs = [], [], [], []
    for l in range(DEPTH):
        p = layer_params(l)
        xm, s_meta, buf_meta = _layer(
            xm, jnp.zeros((1, HEADS, DK, DV), F32), jnp.zeros((1, CONV_K - 1, D_MODEL), F32), p,
            seq_len=N_META, chunk=N_META, tm=N_META, sb=1, tb=N_META, shared_s0=True)
        xp, s_p, buf_p = _layer(
            xp, s_meta, jnp.broadcast_to(buf_meta, (bp, CONV_K - 1, D_MODEL)), p,
            seq_len=seq, chunk=128, tm=256, sb=1, tb=256, shared_s0=True)
        xs, s_s, buf_s = _layer(
            xs, state_hgrn[l], state_conv[l], p,
            seq_len=dec_seq, chunk=dec_seq, tm=256, sb=8, tb=dec_seq, shared_s0=False)
        hp.append(s_p)
        cp.append(buf_p)
        hs.append(s_s)
        cs.append(buf_s)
    return (xp.reshape(bp, seq, D_MODEL), xs.reshape(bs, dec_seq, D_MODEL),
            jnp.stack(hp), jnp.stack(cp), jnp.stack(hs), jnp.stack(cs))
```

```python
import functools
import math

import jax
import jax.numpy as jnp
from jax import lax
from jax.experimental import pallas as pl
from jax.experimental.pallas import tpu as pltpu

D_MODEL = 1024
DEPTH = 4
N_META = 16
HEADS = 8
DK = 128
DV = 128
CONV_K = 3
D_FF = 2816
N_SPLITS = 9
ALPHA = (2 * DEPTH) ** 0.25
LN_EPS = 1e-5
RMS_EPS = 1e-6
LOG2E = math.log2(math.e)

LANES = 128
SUBLANES = 8
VMEM_LIMIT_BYTES = 56 * 1024 * 1024

F32 = jnp.float32
BF16 = jnp.bfloat16

_NT = (((1,), (1,)), ((), ()))
_TN = (((0,), (0,)), ((), ()))


def _const_spec(shape):
    nd = len(shape)
    return pl.BlockSpec(shape, lambda *_: (0,) * nd, pipeline_mode=pl.Buffered(1))


def _layernorm(y, g, b):
    mu = jnp.mean(y, axis=-1, keepdims=True)
    d = y - mu
    var = jnp.mean(d * d, axis=-1, keepdims=True)
    return d * lax.rsqrt(var + LN_EPS) * g + b


def _sigmoid(x):
    return 1.0 / (1.0 + jnp.exp(-x))


def _silu(x):
    return x * _sigmoid(x)


def _lb_kernel(logits_ref, loglb_ref, log1m_ref, om_ref):
    x = logits_ref[...]
    rows = [x[l:l + 1, :] for l in range(DEPTH)]
    m = functools.reduce(jnp.maximum, rows)
    e = [jnp.exp(r - m) for r in rows]
    s = functools.reduce(lambda a, c: a + c, e)
    cs = []
    acc = None
    for l in range(DEPTH):
        p = e[l] / s
        acc = p if acc is None else acc + p
        cs.append(acc)
    for l in range(DEPTH):
        lb = cs[l] - cs[0]
        loglb_ref[l:l + 1, :] = jnp.log(lb)
        log1m_ref[l:l + 1, :] = jnp.log1p(-lb)
        om_ref[l:l + 1, :] = 1.0 - lb


def _lower_bounds(logits):
    shp = jax.ShapeDtypeStruct(logits.shape, F32)
    return pl.pallas_call(_lb_kernel, out_shape=(shp, shp, shp), name="lb_prep")(logits)


def _ffn_ln_kernel(x_ref, wgu_ref, wd_ref, g_ref, b_ref, o_ref):
    x = x_ref[...]
    gu = jnp.dot(x.astype(BF16), wgu_ref[...], preferred_element_type=F32)
    h = _silu(gu[:, :D_FF]) * gu[:, D_FF:]
    y = jnp.dot(h.astype(BF16), wd_ref[...], preferred_element_type=F32)
    o_ref[...] = _layernorm(ALPHA * x + 0.5 * y, g_ref[...], b_ref[...])


def _ffn_ln(x, wgu, wd, g, b, tm):
    n = x.shape[0]
    row = pl.BlockSpec((tm, D_MODEL), lambda i: (i, 0))
    return pl.pallas_call(
        _ffn_ln_kernel,
        out_shape=jax.ShapeDtypeStruct((n, D_MODEL), F32),
        grid=(n // tm,),
        in_specs=[row, _const_spec(wgu.shape), _const_spec(wd.shape),
                  _const_spec(g.shape), _const_spec(b.shape)],
        out_specs=row,
        compiler_params=pltpu.CompilerParams(
            dimension_semantics=("parallel",), vmem_limit_bytes=VMEM_LIMIT_BYTES),
        name="ffn_ln",
    )(x, wgu, wd, g, b)


def _inproj_kernel(x_ref, w_ref, loglb_ref, log1m_ref, om_ref,
                   qs_ref, bc_ref, kk_ref, v_ref, sog_ref, bg_ref, u_ref, sa_ref, sb_ref,
                   *, tm, chunk):
    xb = x_ref[...].astype(BF16)

    def z(j):
        return jnp.dot(xb, w_ref[:, j * D_MODEL:(j + 1) * D_MODEL], preferred_element_type=F32)

    qs_ref[...] = _silu(z(0))

    fp = z(1)
    e = jnp.exp(-jnp.abs(fp))
    log_sig = jnp.minimum(fp, 0.0) - jnp.log1p(e)
    a = loglb_ref[...]
    c = log1m_ref[...] + log_sig
    log2_f = (jnp.maximum(a, c) + jnp.log1p(jnp.exp(-jnp.abs(a - c)))) * LOG2E
    kk_ref[...] = om_ref[...] * (jnp.where(fp >= 0.0, e, 1.0) / (1.0 + e))

    sub = min(tm, LANES)
    r = lax.broadcasted_iota(jnp.int32, (sub, sub), 0)
    cidx = lax.broadcasted_iota(jnp.int32, (sub, sub), 1)
    tri = jnp.where((cidx <= r) & (r // chunk == cidx // chunk), 1.0, 0.0).astype(BF16)
    for s in range(tm // sub):
        g = log2_f[s * sub:(s + 1) * sub, :]
        g_hi = g.astype(BF16)
        g_lo = (g - g_hi.astype(F32)).astype(BF16)
        cum = jnp.dot(tri, jnp.concatenate([g_hi, g_lo], axis=1), preferred_element_type=F32)
        bc_ref[s * sub:(s + 1) * sub, :] = cum[:, :D_MODEL] + cum[:, D_MODEL:]

    v_ref[...] = z(2)
    sog_ref[...] = _silu(z(3)).astype(sog_ref.dtype)
    bg_ref[...] = z(4).astype(bg_ref.dtype)
    u_ref[...] = z(5) * z(6)
    sa_ref[...] = _sigmoid(z(7)).astype(sa_ref.dtype)
    sb_ref[...] = _sigmoid(z(8)).astype(sb_ref.dtype)


def _inproj(x, w_in, loglb, log1m, om, tm, chunk, gate_dtype):
    n = x.shape[0]
    row = pl.BlockSpec((tm, D_MODEL), lambda i: (i, 0))
    vec = _const_spec((1, D_MODEL))
    f = jax.ShapeDtypeStruct((n, D_MODEL), F32)
    g = jax.ShapeDtypeStruct((n, D_MODEL), gate_dtype)
    return pl.pallas_call(
        functools.partial(_inproj_kernel, tm=tm, chunk=chunk),
        out_shape=(f, f, f, f, g, g, f, g, g),
        grid=(n // tm,),
        in_specs=[row, _const_spec(w_in.shape), vec, vec, vec],
        out_specs=(row,) * N_SPLITS,
        compiler_params=pltpu.CompilerParams(
            dimension_semantics=("parallel",), vmem_limit_bytes=VMEM_LIMIT_BYTES),
        name="inproj",
    )(x, w_in, loglb, log1m, om)


def _cross_block_scores(q, k, b, chunk):
    row = lax.broadcasted_iota(jnp.int32, (chunk, chunk), 0)
    col = lax.broadcasted_iota(jnp.int32, (chunk, chunk), 1)
    scores = None
    half = chunk // 2
    while half >= SUBLANES:
        nb = chunk // (2 * half)
        q3 = q.reshape(nb, 2 * half, LANES)
        k3 = k.reshape(nb, 2 * half, LANES)
        b3 = b.reshape(nb, 2 * half, LANES)
        bref = b3[:, half - 1:half, :]
        qu = q3[:, half:, :] * jnp.exp2(b3[:, half:, :] - bref)
        kl = k3[:, :half, :] * jnp.exp2(bref - b3[:, :half, :])
        zero = jnp.zeros((nb, half, LANES), BF16)
        qh = jnp.concatenate([zero, qu.astype(BF16)], axis=1).reshape(chunk, LANES)
        kh = jnp.concatenate([kl.astype(BF16), zero], axis=1).reshape(chunk, LANES)
        part = lax.dot_general(qh, kh, _NT, preferred_element_type=F32)
        if nb > 1:
            part = jnp.where(row // (2 * half) == col // (2 * half), part, 0.0)
        scores = part if scores is None else scores + part
        half //= 2
    return scores


def _hgrn_kernel(qs_ref, bc_ref, kk_ref, v_ref, sog_ref, s0_ref, nw_ref, o_ref, sn_ref, st_ref,
                 *, tb, chunk):
    j = pl.program_id(1)

    @pl.when(j == 0)
    def _():
        for h in range(HEADS):
            st_ref[h] = s0_ref[0, h].T

    nw = nw_ref[...]
    t_idx = lax.broadcasted_iota(jnp.int32, (SUBLANES, 1), 0)

    def head_body(h, carry):
        lanes = pl.ds(pl.multiple_of(h * LANES, LANES), LANES)
        st = st_ref[h]
        for c in range(tb // chunk):
            row0 = c * chunk
            rows = pl.ds(row0, chunk)
            q = qs_ref[0, rows, lanes]
            k = kk_ref[0, rows, lanes]
            v = v_ref[0, rows, lanes]
            b = bc_ref[0, rows, lanes]
            vb = v.astype(BF16)

            bl = b[chunk - 1:chunk, :]
            o = lax.dot_general((q * jnp.exp2(b)).astype(BF16), st.astype(BF16), _NT,
                                preferred_element_type=F32)
            kd = k * jnp.exp2(bl - b)
            st = st * jnp.exp2(bl) + lax.dot_general(vb, kd.astype(BF16), _TN, preferred_element_type=F32)

            if chunk >= 2 * SUBLANES:
                scores = _cross_block_scores(q, k, b, chunk)
                o = o + jnp.dot(scores.astype(BF16), vb, preferred_element_type=F32)

            blocks = []
            for blk in range(chunk // SUBLANES):
                r = row0 + blk * SUBLANES
                qb = q[blk * SUBLANES:(blk + 1) * SUBLANES, :]
                bb = b[blk * SUBLANES:(blk + 1) * SUBLANES, :]
                acc = None
                for s in range(SUBLANES):
                    ks = kk_ref[0, pl.ds(r + s, 1), lanes]
                    bs = bc_ref[0, pl.ds(r + s, 1), lanes]
                    vs = v_ref[0, pl.ds(r + s, 1), lanes]
                    a = jnp.sum(qb * ks * jnp.exp2(bb - bs), axis=-1, keepdims=True)
                    term = jnp.where(t_idx >= s, a, 0.0) * vs
                    acc = term if acc is None else acc + term
                blocks.append(acc)
            o = o + jnp.concatenate(blocks, axis=0)

            o = o * lax.rsqrt(jnp.mean(o * o, axis=-1, keepdims=True) + RMS_EPS) * nw
            o_ref[0, rows, lanes] = (o * sog_ref[0, rows, lanes].astype(F32)).astype(o_ref.dtype)
        st_ref[h] = st
        return carry

    lax.fori_loop(0, HEADS, head_body, 0)

    @pl.when(j == pl.num_programs(1) - 1)
    def _():
        for h in range(HEADS):
            sn_ref[0, h] = st_ref[h].T


def _hgrn(qs, bc, kk, v, sog, s0, nw, *, seq_len, chunk, tb, shared_s0, out_dtype):
    n = qs.shape[0]
    nseq = n // seq_len
    r3 = lambda a: a.reshape(nseq, seq_len, D_MODEL)
    row = pl.BlockSpec((1, tb, D_MODEL), lambda i, j: (i, j, 0))
    if shared_s0:
        s0_spec = pl.BlockSpec((1, HEADS, DK, DV), lambda i, j: (0, 0, 0, 0))
    else:
        s0_spec = pl.BlockSpec((1, HEADS, DK, DV), lambda i, j: (i, 0, 0, 0))
    o, s_new = pl.pallas_call(
        functools.partial(_hgrn_kernel, tb=tb, chunk=chunk),
        out_shape=(jax.ShapeDtypeStruct((nseq, seq_len, D_MODEL), out_dtype),
                   jax.ShapeDtypeStruct((nseq, HEADS, DK, DV), F32)),
        grid=(nseq, seq_len // tb),
        in_specs=[row] * 5 + [s0_spec, _const_spec((1, DV))],
        out_specs=(row, pl.BlockSpec((1, HEADS, DK, DV), lambda i, j: (i, 0, 0, 0))),
        scratch_shapes=[pltpu.VMEM((HEADS, DV, DK), F32)],
        compiler_params=pltpu.CompilerParams(
            dimension_semantics=("parallel", "arbitrary"), vmem_limit_bytes=VMEM_LIMIT_BYTES),
        name="hgrn",
    )(r3(qs), r3(bc), r3(kk), r3(v), r3(sog), s0, nw)
    return o.reshape(n, D_MODEL), s_new


def _hgrn_step_kernel(qs_ref, bc_ref, kk_ref, v_ref, sog_ref, s0_ref, nw_ref, o_ref, sn_ref, *, sb, seq_len):
    nw = nw_ref[...]
    t_idx = lax.broadcasted_iota(jnp.int32, (seq_len, 1), 0)

    def seq_body(s, carry):
        for h in range(HEADS):
            lanes = slice(h * LANES, (h + 1) * LANES)
            q = qs_ref[s, :, lanes]
            k = kk_ref[s, :, lanes]
            v = v_ref[s, :, lanes]
            b = bc_ref[s, :, lanes]
            st = s0_ref[s, h]
            bl = b[seq_len - 1:seq_len, :]
            o = jnp.dot((q * jnp.exp2(b)).astype(BF16), st.astype(BF16), preferred_element_type=F32)
            kd = k * jnp.exp2(bl - b)
            upd = lax.dot_general(kd.astype(BF16), v.astype(BF16), _TN, preferred_element_type=F32)
            decay_col = jnp.broadcast_to(jnp.exp2(bl), (SUBLANES, LANES)).T[:, 0:1]
            sn_ref[s, h] = st * decay_col + upd
            for r in range(seq_len):
                a = jnp.sum(q * k[r:r + 1, :] * jnp.exp2(b - b[r:r + 1, :]), axis=-1, keepdims=True)
                o = o + jnp.where(t_idx >= r, a, 0.0) * v[r:r + 1, :]
            o = o * lax.rsqrt(jnp.mean(o * o, axis=-1, keepdims=True) + RMS_EPS) * nw
            o_ref[s, :, lanes] = o * sog_ref[s, :, lanes]
        return carry

    lax.fori_loop(0, sb, seq_body, 0)


def _hgrn_step(qs, bc, kk, v, sog, s0, nw, *, seq_len, sb):
    n = qs.shape[0]
    nseq = n // seq_len
    r3 = lambda a: a.reshape(nseq, seq_len, D_MODEL)
    row = pl.BlockSpec((sb, seq_len, D_MODEL), lambda i: (i, 0, 0))
    st = pl.BlockSpec((sb, HEADS, DK, DV), lambda i: (i, 0, 0, 0))
    o, s_new = pl.pallas_call(
        functools.partial(_hgrn_step_kernel, sb=sb, seq_len=seq_len),
        out_shape=(jax.ShapeDtypeStruct((nseq, seq_len, D_MODEL), F32),
                   jax.ShapeDtypeStruct((nseq, HEADS, DK, DV), F32)),
        grid=(nseq // sb,),
        in_specs=[row] * 5 + [st, _const_spec((1, DV))],
        out_specs=(row, st),
        compiler_params=pltpu.CompilerParams(
            dimension_semantics=("parallel",), vmem_limit_bytes=VMEM_LIMIT_BYTES),
        name="hgrn_step",
    )(r3(qs), r3(bc), r3(kk), r3(v), r3(sog), s0, nw)
    return o.reshape(n, D_MODEL), s_new


def _mixer_tail(o, conv, bg, sa, sb, x, wa_ref, wb_ref, wo_ref, g_ref, b_ref):
    y_b = jnp.dot((bg.astype(F32) * conv).astype(BF16), wb_ref[...], preferred_element_type=F32)
    y_a = jnp.dot(o.astype(BF16), wa_ref[...], preferred_element_type=F32)
    m = sa.astype(F32) * y_a + sb.astype(F32) * y_b
    out = jnp.dot(m.astype(BF16), wo_ref[...], preferred_element_type=F32)
    return _layernorm(ALPHA * x + out, g_ref[...], b_ref[...])


def _mixer_out_kernel(o_ref, um2_ref, um1_ref, u_ref, bg_ref, sa_ref, sb_ref, x_ref,
                      cw_ref, wa_ref, wb_ref, wo_ref, g_ref, b_ref, out_ref):
    conv = cw_ref[0:1, :] * um2_ref[...] + cw_ref[1:2, :] * um1_ref[...] + cw_ref[2:3, :] * u_ref[...]
    out_ref[...] = _mixer_tail(o_ref[...], conv, bg_ref[...], sa_ref[...], sb_ref[...], x_ref[...],
                               wa_ref, wb_ref, wo_ref, g_ref, b_ref)


def _mixer_out(o, um2, um1, u, bg, sa, sb, x, cw, wa, wb, wo, g, b, tm):
    n = x.shape[0]
    row = pl.BlockSpec((tm, D_MODEL), lambda i: (i, 0))
    wspec = _const_spec((D_MODEL, D_MODEL))
    vec = _const_spec((1, D_MODEL))
    return pl.pallas_call(
        _mixer_out_kernel,
        out_shape=jax.ShapeDtypeStruct((n, D_MODEL), F32),
        grid=(n // tm,),
        in_specs=[row] * 8 + [_const_spec((CONV_K, D_MODEL)), wspec, wspec, wspec, vec, vec],
        out_specs=row,
        compiler_params=pltpu.CompilerParams(
            dimension_semantics=("parallel",), vmem_limit_bytes=VMEM_LIMIT_BYTES),
        name="mixer_out",
    )(o, um2, um1, u, bg, sa, sb, x, cw, wa, wb, wo, g, b)


def _mixer_out_seq_kernel(o_ref, u_ref, halo_ref, buf_ref, bg_ref, sa_ref, sb_ref, x_ref,
                          cw_ref, wa_ref, wb_ref, wo_ref, g_ref, b_ref, out_ref, *, tm, tiles_per_seq):
    u = u_ref[...]
    first = (pl.program_id(0) % tiles_per_seq) == 0
    p2 = jnp.where(first, buf_ref[0:1, :], halo_ref[SUBLANES - 2:SUBLANES - 1, :])
    p1 = jnp.where(first, buf_ref[1:2, :], halo_ref[SUBLANES - 1:SUBLANES, :])
    row = lax.broadcasted_iota(jnp.int32, (tm, 1), 0)
    um1 = jnp.where(row == 0, p1, pltpu.roll(u, 1, 0))
    um2 = jnp.where(row == 0, p2, jnp.where(row == 1, p1, pltpu.roll(u, 2, 0)))
    conv = cw_ref[0:1, :] * um2 + cw_ref[1:2, :] * um1 + cw_ref[2:3, :] * u
    out_ref[...] = _mixer_tail(o_ref[...], conv, bg_ref[...], sa_ref[...], sb_ref[...], x_ref[...],
                               wa_ref, wb_ref, wo_ref, g_ref, b_ref)


def _mixer_out_seq(o, u, buf, bg, sa, sb, x, cw, wa, wb, wo, g, b, tm, seq_len):
    n = x.shape[0]
    row = pl.BlockSpec((tm, D_MODEL), lambda i: (i, 0))
    halo = pl.BlockSpec((SUBLANES, D_MODEL), lambda i: (jnp.maximum(i * (tm // SUBLANES) - 1, 0), 0))
    wspec = _const_spec((D_MODEL, D_MODEL))
    vec = _const_spec((1, D_MODEL))
    return pl.pallas_call(
        functools.partial(_mixer_out_seq_kernel, tm=tm, tiles_per_seq=seq_len // tm),
        out_shape=jax.ShapeDtypeStruct((n, D_MODEL), F32),
        grid=(n // tm,),
        in_specs=[row, row, halo, _const_spec((CONV_K - 1, D_MODEL)), row, row, row, row,
                  _const_spec((CONV_K, D_MODEL)), wspec, wspec, wspec, vec, vec],
        out_specs=row,
        compiler_params=pltpu.CompilerParams(
            dimension_semantics=("parallel",), vmem_limit_bytes=VMEM_LIMIT_BYTES),
        name="mixer_out_seq",
    )(o, u, u, buf, bg, sa, sb, x, cw, wa, wb, wo, g, b)


def _layer_short(x, s0, buf0, p, *, seq_len, tm, long_kernel):
    n = x.shape[0]
    nseq = n // seq_len
    x = _ffn_ln(x, p["ffn1_wgu"], p["ffn1_wd"], p["ln_g"][0], p["ln_b"][0], tm)
    qs, bc, kk, v, sog, bg, u, sa, sbg = _inproj(x, p["w_in"], p["loglb"], p["log1m"], p["om"], tm, seq_len, F32)
    if long_kernel:
        o, s_new = _hgrn(qs, bc, kk, v, sog, s0, p["norm_w"], seq_len=seq_len, chunk=seq_len, tb=seq_len,
                         shared_s0=False, out_dtype=F32)
    else:
        o, s_new = _hgrn_step(qs, bc, kk, v, sog, s0, p["norm_w"], seq_len=seq_len, sb=8)
    uu = jnp.concatenate([buf0, u.reshape(nseq, seq_len, D_MODEL)], axis=1)
    um2 = uu[:, 0:seq_len].reshape(n, D_MODEL)
    um1 = uu[:, 1:seq_len + 1].reshape(n, D_MODEL)
    x = _mixer_out(o, um2, um1, u, bg, sa, sbg, x, p["conv_w"], p["w_a"], p["w_b"], p["w_o"],
                   p["ln_g"][1], p["ln_b"][1], tm)
    x = _ffn_ln(x, p["ffn2_wgu"], p["ffn2_wd"], p["ln_g"][2], p["ln_b"][2], tm)
    return x, s_new, uu[:, seq_len:]


def _layer_long(x, s0, buf0, p, *, seq_len, tm, tb):
    n = x.shape[0]
    nseq = n // seq_len
    x = _ffn_ln(x, p["ffn1_wgu"], p["ffn1_wd"], p["ln_g"][0], p["ln_b"][0], tm)
    qs, bc, kk, v, sog, bg, u, sa, sbg = _inproj(x, p["w_in"], p["loglb"], p["log1m"], p["om"], tm, LANES, BF16)
    o, s_new = _hgrn(qs, bc, kk, v, sog, s0, p["norm_w"], seq_len=seq_len, chunk=LANES, tb=tb,
                     shared_s0=True, out_dtype=BF16)
    x = _mixer_out_seq(o, u, buf0, bg, sa, sbg, x, p["conv_w"], p["w_a"], p["w_b"], p["w_o"],
                       p["ln_g"][1], p["ln_b"][1], tm, seq_len)
    x = _ffn_ln(x, p["ffn2_wgu"], p["ffn2_wd"], p["ln_g"][2], p["ln_b"][2], tm)
    return x, s_new, u.reshape(nseq, seq_len, D_MODEL)[:, seq_len - (CONV_K - 1):]


def kernel(x_prompt, x_sample, state_hgrn, state_conv, meta_tokens, w_in, hgrn_lb_logits, hgrn_norm_w,
           conv_w, w_branch_a, w_branch_b, w_out, ffn1_w_gu, ffn1_w_down, ffn2_w_gu, ffn2_w_down,
           ln_g, ln_b):
    bp, seq, _ = x_prompt.shape
    bs, dec_seq, _ = x_sample.shape
    loglb, log1m, om = _lower_bounds(hgrn_lb_logits.astype(F32))

    def layer_params(l):
        return dict(
            w_in=w_in[l].astype(BF16), loglb=loglb[l:l + 1], log1m=log1m[l:l + 1], om=om[l:l + 1],
            norm_w=hgrn_norm_w[l][None, :], conv_w=conv_w[l],
            w_a=w_branch_a[l].astype(BF16), w_b=w_branch_b[l].astype(BF16), w_o=w_out[l].astype(BF16),
            ffn1_wgu=ffn1_w_gu[l].astype(BF16), ffn1_wd=ffn1_w_down[l].astype(BF16),
            ffn2_wgu=ffn2_w_gu[l].astype(BF16), ffn2_wd=ffn2_w_down[l].astype(BF16),
            ln_g=[ln_g[l, i][None, :] for i in range(3)], ln_b=[ln_b[l, i][None, :] for i in range(3)])

    xm = meta_tokens.astype(F32)
    xp = x_prompt.reshape(bp * seq, D_MODEL)
    xs = x_sample.reshape(bs * dec_seq, D_MODEL)
    hp, cp, hs, cs = [], [], [], []
    for l in range(DEPTH):
        p = layer_params(l)
        xm, s_meta, buf_meta = _layer_short(
            xm, jnp.zeros((1, HEADS, DK, DV), F32), jnp.zeros((1, CONV_K - 1, D_MODEL), F32), p,
            seq_len=N_META, tm=N_META, long_kernel=True)
        xp, s_p, buf_p = _layer_long(xp, s_meta, buf_meta[0], p, seq_len=seq, tm=256, tb=256)
        xs, s_s, buf_s = _layer_short(xs, state_hgrn[l], state_conv[l], p,
                                      seq_len=dec_seq, tm=256, long_kernel=False)
        hp.append(s_p)
        cp.append(buf_p)
        hs.append(s_s)
        cs.append(buf_s)
    return (xp.reshape(bp, seq, D_MODEL), xs.reshape(bs, dec_seq, D_MODEL),
            jnp.stack(hp), jnp.stack(cp), jnp.stack(hs), jnp.stack(cs))
```

```python
import functools
import math

import jax
import jax.numpy as jnp
from jax import lax
from jax.experimental import pallas as pl
from jax.experimental.pallas import tpu as pltpu

D_MODEL = 1024
DEPTH = 4
N_META = 16
HEADS = 8
DK = 128
DV = 128
CONV_K = 3
D_FF = 2816
N_SPLITS = 9
ALPHA = (2 * DEPTH) ** 0.25
LN_EPS = 1e-5
RMS_EPS = 1e-6
LOG2E = math.log2(math.e)

LANES = 128
SUBLANES = 8
VMEM_LIMIT_BYTES = 56 * 1024 * 1024

F32 = jnp.float32
BF16 = jnp.bfloat16

_NT = (((1,), (1,)), ((), ()))
_TN = (((0,), (0,)), ((), ()))


def _const_spec(shape):
    nd = len(shape)
    return pl.BlockSpec(shape, lambda *_: (0,) * nd, pipeline_mode=pl.Buffered(1))


def _layernorm(y, g, b):
    mu = jnp.mean(y, axis=-1, keepdims=True)
    d = y - mu
    var = jnp.mean(d * d, axis=-1, keepdims=True)
    return d * lax.rsqrt(var + LN_EPS) * g + b


def _sigmoid(x):
    return 1.0 / (1.0 + jnp.exp(-x))


def _silu(x):
    return x * _sigmoid(x)


def _lb_kernel(logits_ref, loglb_ref, log1m_ref, om_ref):
    x = logits_ref[...]
    rows = [x[l:l + 1, :] for l in range(DEPTH)]
    m = functools.reduce(jnp.maximum, rows)
    e = [jnp.exp(r - m) for r in rows]
    s = functools.reduce(lambda a, c: a + c, e)
    cs = []
    acc = None
    for l in range(DEPTH):
        p = e[l] / s
        acc = p if acc is None else acc + p
        cs.append(acc)
    for l in range(DEPTH):
        lb = cs[l] - cs[0]
        loglb_ref[l:l + 1, :] = jnp.log(lb)
        log1m_ref[l:l + 1, :] = jnp.log1p(-lb)
        om_ref[l:l + 1, :] = 1.0 - lb


def _lower_bounds(logits):
    shp = jax.ShapeDtypeStruct(logits.shape, F32)
    return pl.pallas_call(_lb_kernel, out_shape=(shp, shp, shp), name="lb_prep")(logits)


def _ffn_ln_kernel(x_ref, wgu_ref, wd_ref, g_ref, b_ref, o_ref):
    x = x_ref[...]
    gu = jnp.dot(x.astype(BF16), wgu_ref[...], preferred_element_type=F32)
    h = _silu(gu[:, :D_FF]) * gu[:, D_FF:]
    y = jnp.dot(h.astype(BF16), wd_ref[...], preferred_element_type=F32)
    o_ref[...] = _layernorm(ALPHA * x + 0.5 * y, g_ref[...], b_ref[...])


def _ffn_ln(x, wgu, wd, g, b, tm):
    n = x.shape[0]
    row = pl.BlockSpec((tm, D_MODEL), lambda i: (i, 0))
    return pl.pallas_call(
        _ffn_ln_kernel,
        out_shape=jax.ShapeDtypeStruct((n, D_MODEL), F32),
        grid=(n // tm,),
        in_specs=[row, _const_spec(wgu.shape), _const_spec(wd.shape),
                  _const_spec(g.shape), _const_spec(b.shape)],
        out_specs=row,
        compiler_params=pltpu.CompilerParams(
            dimension_semantics=("parallel",), vmem_limit_bytes=VMEM_LIMIT_BYTES),
        name="ffn_ln",
    )(x, wgu, wd, g, b)


def _forget_terms(fp, loglb, log1m, om):
    e = jnp.exp(-jnp.abs(fp))
    log_sig = jnp.minimum(fp, 0.0) - jnp.log1p(e)
    c = log1m + log_sig
    log2_f = (jnp.maximum(loglb, c) + jnp.log1p(jnp.exp(-jnp.abs(loglb - c)))) * LOG2E
    k = om * (jnp.where(fp >= 0.0, e, 1.0) / (1.0 + e))
    return log2_f, k


def _chunk_cumsum(log2_f, bc_ref, tm, chunk):
    sub = min(tm, LANES)
    r = lax.broadcasted_iota(jnp.int32, (sub, sub), 0)
    cidx = lax.broadcasted_iota(jnp.int32, (sub, sub), 1)
    tri = jnp.where((cidx <= r) & (r // chunk == cidx // chunk), 1.0, 0.0).astype(BF16)
    for s in range(tm // sub):
        g = log2_f[s * sub:(s + 1) * sub, :]
        g_hi = g.astype(BF16)
        g_lo = (g - g_hi.astype(F32)).astype(BF16)
        cum = jnp.dot(tri, jnp.concatenate([g_hi, g_lo], axis=1), preferred_element_type=F32)
        bc_ref[s * sub:(s + 1) * sub, :] = cum[:, :D_MODEL] + cum[:, D_MODEL:]


def _inproj_kernel(x_ref, w_ref, loglb_ref, log1m_ref, om_ref,
                   qs_ref, bc_ref, kk_ref, v_ref, sog_ref, bg_ref, u_ref, sa_ref, sb_ref,
                   *, tm, chunk):
    xb = x_ref[...].astype(BF16)

    def z(j):
        return jnp.dot(xb, w_ref[:, j * D_MODEL:(j + 1) * D_MODEL], preferred_element_type=F32)

    qs_ref[...] = _silu(z(0))
    log2_f, k = _forget_terms(z(1), loglb_ref[...], log1m_ref[...], om_ref[...])
    kk_ref[...] = k
    _chunk_cumsum(log2_f, bc_ref, tm, chunk)
    v_ref[...] = z(2)
    sog_ref[...] = _silu(z(3))
    bg_ref[...] = z(4)
    u_ref[...] = z(5) * z(6)
    sa_ref[...] = _sigmoid(z(7))
    sb_ref[...] = _sigmoid(z(8))


def _inproj(x, w_in, loglb, log1m, om, tm, chunk):
    n = x.shape[0]
    row = pl.BlockSpec((tm, D_MODEL), lambda i: (i, 0))
    vec = _const_spec((1, D_MODEL))
    f = jax.ShapeDtypeStruct((n, D_MODEL), F32)
    return pl.pallas_call(
        functools.partial(_inproj_kernel, tm=tm, chunk=chunk),
        out_shape=(f,) * N_SPLITS,
        grid=(n // tm,),
        in_specs=[row, _const_spec(w_in.shape), vec, vec, vec],
        out_specs=(row,) * N_SPLITS,
        compiler_params=pltpu.CompilerParams(
            dimension_semantics=("parallel",), vmem_limit_bytes=VMEM_LIMIT_BYTES),
        name="inproj",
    )(x, w_in, loglb, log1m, om)


def _cross_block_scores(q, k, b, chunk):
    row = lax.broadcasted_iota(jnp.int32, (chunk, chunk), 0)
    col = lax.broadcasted_iota(jnp.int32, (chunk, chunk), 1)
    scores = None
    half = chunk // 2
    while half >= SUBLANES:
        nb = chunk // (2 * half)
        q3 = q.reshape(nb, 2 * half, LANES)
        k3 = k.reshape(nb, 2 * half, LANES)
        b3 = b.reshape(nb, 2 * half, LANES)
        bref = b3[:, half - 1:half, :]
        qu = q3[:, half:, :] * jnp.exp2(b3[:, half:, :] - bref)
        kl = k3[:, :half, :] * jnp.exp2(bref - b3[:, :half, :])
        zero = jnp.zeros((nb, half, LANES), BF16)
        qh = jnp.concatenate([zero, qu.astype(BF16)], axis=1).reshape(chunk, LANES)
        kh = jnp.concatenate([kl.astype(BF16), zero], axis=1).reshape(chunk, LANES)
        part = lax.dot_general(qh, kh, _NT, preferred_element_type=F32)
        if nb > 1:
            part = jnp.where(row // (2 * half) == col // (2 * half), part, 0.0)
        scores = part if scores is None else scores + part
        half //= 2
    return scores


def _hgrn_chunk(tile, row_of, st, chunk, row0):
    rows = pl.ds(row0, chunk)
    q = tile("q", rows)
    k = tile("k", rows)
    v = tile("v", rows)
    b = tile("b", rows)
    vb = v.astype(BF16)

    bl = b[chunk - 1:chunk, :]
    o = lax.dot_general((q * jnp.exp2(b)).astype(BF16), st.astype(BF16), _NT, preferred_element_type=F32)
    kd = k * jnp.exp2(bl - b)
    st = st * jnp.exp2(bl) + lax.dot_general(vb, kd.astype(BF16), _TN, preferred_element_type=F32)

    if chunk >= 2 * SUBLANES:
        scores = _cross_block_scores(q, k, b, chunk)
        o = o + jnp.dot(scores.astype(BF16), vb, preferred_element_type=F32)

    t_idx = lax.broadcasted_iota(jnp.int32, (SUBLANES, 1), 0)
    blocks = []
    for blk in range(chunk // SUBLANES):
        r = row0 + blk * SUBLANES
        qb = q[blk * SUBLANES:(blk + 1) * SUBLANES, :]
        bb = b[blk * SUBLANES:(blk + 1) * SUBLANES, :]
        acc = None
        for s in range(SUBLANES):
            a = jnp.sum(qb * row_of("k", r + s) * jnp.exp2(bb - row_of("b", r + s)), axis=-1, keepdims=True)
            term = jnp.where(t_idx >= s, a, 0.0) * row_of("v", r + s)
            acc = term if acc is None else acc + term
        blocks.append(acc)
    return o + jnp.concatenate(blocks, axis=0), st


def _rms_gate(o, nw, gate):
    return o * lax.rsqrt(jnp.mean(o * o, axis=-1, keepdims=True) + RMS_EPS) * nw * gate


def _hgrn_kernel(qs_ref, bc_ref, kk_ref, v_ref, sog_ref, s0_ref, nw_ref, o_ref, sn_ref, *, seq_len):
    nw = nw_ref[...]
    refs = dict(q=qs_ref, k=kk_ref, v=v_ref, b=bc_ref)
    for h in range(HEADS):
        lanes = slice(h * LANES, (h + 1) * LANES)
        o, st = _hgrn_chunk(lambda n, rows: refs[n][0, rows, lanes],
                            lambda n, r: refs[n][0, pl.ds(r, 1), lanes],
                            s0_ref[0, h].T, seq_len, 0)
        sn_ref[0, h] = st.T
        o_ref[0, :, lanes] = _rms_gate(o, nw, sog_ref[0, :, lanes])


def _hgrn(qs, bc, kk, v, sog, s0, nw, *, seq_len):
    n = qs.shape[0]
    nseq = n // seq_len
    r3 = lambda a: a.reshape(nseq, seq_len, D_MODEL)
    row = pl.BlockSpec((1, seq_len, D_MODEL), lambda i: (i, 0, 0))
    st = pl.BlockSpec((1, HEADS, DK, DV), lambda i: (i, 0, 0, 0))
    o, s_new = pl.pallas_call(
        functools.partial(_hgrn_kernel, seq_len=seq_len),
        out_shape=(jax.ShapeDtypeStruct((nseq, seq_len, D_MODEL), F32),
                   jax.ShapeDtypeStruct((nseq, HEADS, DK, DV), F32)),
        grid=(nseq,),
        in_specs=[row] * 5 + [st, _const_spec((1, DV))],
        out_specs=(row, st),
        compiler_params=pltpu.CompilerParams(
            dimension_semantics=("parallel",), vmem_limit_bytes=VMEM_LIMIT_BYTES),
        name="hgrn",
    )(r3(qs), r3(bc), r3(kk), r3(v), r3(sog), s0, nw)
    return o.reshape(n, D_MODEL), s_new


def _hgrn_step_kernel(qs_ref, bc_ref, kk_ref, v_ref, sog_ref, s0_ref, nw_ref, *rest, sb, seq_len):
    o_ref, sn_ref = rest[-2:]
    nw = nw_ref[...]
    t_idx = lax.broadcasted_iota(jnp.int32, (seq_len, 1), 0)

    def seq_body(s, carry):
        for h in range(HEADS):
            lanes = slice(h * LANES, (h + 1) * LANES)
            q = qs_ref[s, :, lanes]
            k = kk_ref[s, :, lanes]
            v = v_ref[s, :, lanes]
            b = bc_ref[s, :, lanes]
            st = s0_ref[0, s, h]
            bl = b[seq_len - 1:seq_len, :]
            o = jnp.dot((q * jnp.exp2(b)).astype(BF16), st.astype(BF16), preferred_element_type=F32)
            kd = k * jnp.exp2(bl - b)
            upd = lax.dot_general(kd.astype(BF16), v.astype(BF16), _TN, preferred_element_type=F32)
            decay_col = jnp.broadcast_to(jnp.exp2(bl), (SUBLANES, LANES)).T[:, 0:1]
            sn_ref[0, s, h] = st * decay_col + upd
            for r in range(seq_len):
                a = jnp.sum(q * k[r:r + 1, :] * jnp.exp2(b - b[r:r + 1, :]), axis=-1, keepdims=True)
                o = o + jnp.where(t_idx >= r, a, 0.0) * v[r:r + 1, :]
            o_ref[s, :, lanes] = _rms_gate(o, nw, sog_ref[s, :, lanes])
        return carry

    lax.fori_loop(0, sb, seq_body, 0)


def _hgrn_step(qs, bc, kk, v, sog, s_all, s_new_all, layer, nw, *, seq_len, sb):
    n = qs.shape[0]
    nseq = n // seq_len
    r3 = lambda a: a.reshape(nseq, seq_len, D_MODEL)
    row = pl.BlockSpec((sb, seq_len, D_MODEL), lambda i: (i, 0, 0))
    st = pl.BlockSpec((1, sb, HEADS, DK, DV), lambda i: (layer, i, 0, 0, 0))
    operands = [r3(qs), r3(bc), r3(kk), r3(v), r3(sog), s_all, nw]
    in_specs = [row] * 5 + [st, _const_spec((1, DV))]
    aliases = {}
    if s_new_all is not None:
        operands.append(s_new_all)
        in_specs.append(pl.BlockSpec(memory_space=pl.ANY))
        aliases = {len(operands) - 1: 1}
    o, s_new_all = pl.pallas_call(
        functools.partial(_hgrn_step_kernel, sb=sb, seq_len=seq_len),
        out_shape=(jax.ShapeDtypeStruct((nseq, seq_len, D_MODEL), F32),
                   jax.ShapeDtypeStruct(s_all.shape, F32)),
        grid=(nseq // sb,),
        in_specs=in_specs,
        out_specs=(row, st),
        input_output_aliases=aliases,
        compiler_params=pltpu.CompilerParams(
            dimension_semantics=("parallel",), vmem_limit_bytes=VMEM_LIMIT_BYTES),
        name="hgrn_step",
    )(*operands)
    return o.reshape(n, D_MODEL), s_new_all


def _mixer_tail(o, conv, bg, sa, sb, x, wa_ref, wb_ref, wo_ref, g_ref, b_ref):
    y_b = jnp.dot((bg * conv).astype(BF16), wb_ref[...], preferred_element_type=F32)
    y_a = jnp.dot(o.astype(BF16), wa_ref[...], preferred_element_type=F32)
    m = sa * y_a + sb * y_b
    out = jnp.dot(m.astype(BF16), wo_ref[...], preferred_element_type=F32)
    return _layernorm(ALPHA * x + out, g_ref[...], b_ref[...])


def _mixer_out_kernel(o_ref, um2_ref, um1_ref, u_ref, bg_ref, sa_ref, sb_ref, x_ref,
                      cw_ref, wa_ref, wb_ref, wo_ref, g_ref, b_ref, out_ref):
    conv = cw_ref[0:1, :] * um2_ref[...] + cw_ref[1:2, :] * um1_ref[...] + cw_ref[2:3, :] * u_ref[...]
    out_ref[...] = _mixer_tail(o_ref[...], conv, bg_ref[...], sa_ref[...], sb_ref[...], x_ref[...],
                               wa_ref, wb_ref, wo_ref, g_ref, b_ref)


def _mixer_out(o, um2, um1, u, bg, sa, sb, x, cw, wa, wb, wo, g, b, tm):
    n = x.shape[0]
    row = pl.BlockSpec((tm, D_MODEL), lambda i: (i, 0))
    wspec = _const_spec((D_MODEL, D_MODEL))
    vec = _const_spec((1, D_MODEL))
    return pl.pallas_call(
        _mixer_out_kernel,
        out_shape=jax.ShapeDtypeStruct((n, D_MODEL), F32),
        grid=(n // tm,),
        in_specs=[row] * 8 + [_const_spec((CONV_K, D_MODEL)), wspec, wspec, wspec, vec, vec],
        out_specs=row,
        compiler_params=pltpu.CompilerParams(
            dimension_semantics=("parallel",), vmem_limit_bytes=VMEM_LIMIT_BYTES),
        name="mixer_out",
    )(o, um2, um1, u, bg, sa, sb, x, cw, wa, wb, wo, g, b)


def _mixer_kernel(x_ref, w_ref, loglb_ref, log1m_ref, om_ref, nw_ref, s0_ref, buf0_ref, cw_ref,
                  wa_ref, wb_ref, wo_ref, g_ref, b_ref,
                  out_ref, sn_ref, nbuf_ref,
                  q_s, b_s, k_s, v_s, o_s, st_s, tail_s, *, tm, chunk):
    j = pl.program_id(1)

    @pl.when(j == 0)
    def _():
        for h in range(HEADS):
            st_s[h] = s0_ref[0, h].T
        tail_s[SUBLANES - (CONV_K - 1):SUBLANES, :] = buf0_ref[...]

    x = x_ref[...]
    xb = x.astype(BF16)

    def z(i):
        return jnp.dot(xb, w_ref[:, i * D_MODEL:(i + 1) * D_MODEL], preferred_element_type=F32)

    q_s[...] = _silu(z(0))
    log2_f, k = _forget_terms(z(1), loglb_ref[...], log1m_ref[...], om_ref[...])
    k_s[...] = k
    _chunk_cumsum(log2_f, b_s, tm, chunk)
    v_s[...] = z(2)
    sog = _silu(z(3))
    bg = z(4)
    u = z(5) * z(6)
    sa = _sigmoid(z(7))
    sb = _sigmoid(z(8))

    p2 = tail_s[SUBLANES - 2:SUBLANES - 1, :]
    p1 = tail_s[SUBLANES - 1:SUBLANES, :]
    row = lax.broadcasted_iota(jnp.int32, (tm, 1), 0)
    um1 = jnp.where(row == 0, p1, pltpu.roll(u, 1, 0))
    um2 = jnp.where(row == 0, p2, jnp.where(row == 1, p1, pltpu.roll(u, 2, 0)))
    conv = cw_ref[0:1, :] * um2 + cw_ref[1:2, :] * um1 + cw_ref[2:3, :] * u
    tail_s[...] = u[tm - SUBLANES:tm, :]
    nbuf_ref[0] = u[tm - (CONV_K - 1):tm, :]
    y_b = jnp.dot((bg * conv).astype(BF16), wb_ref[...], preferred_element_type=F32)

    nw = nw_ref[...]
    refs = dict(q=q_s, k=k_s, v=v_s, b=b_s)
    for h in range(HEADS):
        lanes = slice(h * LANES, (h + 1) * LANES)
        st = st_s[h]
        for c in range(tm // chunk):
            o, st = _hgrn_chunk(lambda n, rows: refs[n][rows, lanes],
                                lambda n, r: refs[n][pl.ds(r, 1), lanes],
                                st, chunk, c * chunk)
            o_s[c * chunk:(c + 1) * chunk, lanes] = _rms_gate(
                o, nw, sog[c * chunk:(c + 1) * chunk, lanes]).astype(BF16)
        st_s[h] = st

    y_a = jnp.dot(o_s[...], wa_ref[...], preferred_element_type=F32)
    m = sa * y_a + sb * y_b
    out = jnp.dot(m.astype(BF16), wo_ref[...], preferred_element_type=F32)
    out_ref[...] = _layernorm(ALPHA * x + out, g_ref[...], b_ref[...])

    @pl.when(j == pl.num_programs(1) - 1)
    def _():
        for h in range(HEADS):
            sn_ref[0, h] = st_s[h].T


def _mixer(x, p, s0, buf0, *, seq_len, tm, chunk):
    n = x.shape[0]
    nseq = n // seq_len
    tiles = seq_len // tm
    row = pl.BlockSpec((tm, D_MODEL), lambda i, j: (i * tiles + j, 0))
    vec = _const_spec((1, D_MODEL))
    wspec = _const_spec((D_MODEL, D_MODEL))
    rows_f32 = pltpu.VMEM((tm, D_MODEL), F32)
    return pl.pallas_call(
        functools.partial(_mixer_kernel, tm=tm, chunk=chunk),
        out_shape=(jax.ShapeDtypeStruct((n, D_MODEL), F32),
                   jax.ShapeDtypeStruct((nseq, HEADS, DK, DV), F32),
                   jax.ShapeDtypeStruct((nseq, CONV_K - 1, D_MODEL), F32)),
        grid=(nseq, tiles),
        in_specs=[row, _const_spec(p["w_in"].shape), vec, vec, vec, _const_spec((1, DV)),
                  _const_spec((1, HEADS, DK, DV)), _const_spec((CONV_K - 1, D_MODEL)),
                  _const_spec((CONV_K, D_MODEL)), wspec, wspec, wspec, vec, vec],
        out_specs=(row, pl.BlockSpec((1, HEADS, DK, DV), lambda i, j: (i, 0, 0, 0)),
                   pl.BlockSpec((1, CONV_K - 1, D_MODEL), lambda i, j: (i, 0, 0))),
        scratch_shapes=[rows_f32, rows_f32, rows_f32, rows_f32, pltpu.VMEM((tm, D_MODEL), BF16),
                        pltpu.VMEM((HEADS, DV, DK), F32), pltpu.VMEM((SUBLANES, D_MODEL), F32)],
        compiler_params=pltpu.CompilerParams(
            dimension_semantics=("parallel", "arbitrary"), vmem_limit_bytes=VMEM_LIMIT_BYTES),
        name="mixer",
    )(x, p["w_in"], p["loglb"], p["log1m"], p["om"], p["norm_w"], s0, buf0, p["conv_w"],
      p["w_a"], p["w_b"], p["w_o"], p["ln_g"][1], p["ln_b"][1])


def _layer_short(x, hgrn_fn, buf0, p, *, seq_len, tm):
    n = x.shape[0]
    nseq = n // seq_len
    x = _ffn_ln(x, p["ffn1_wgu"], p["ffn1_wd"], p["ln_g"][0], p["ln_b"][0], tm)
    qs, bc, kk, v, sog, bg, u, sa, sbg = _inproj(x, p["w_in"], p["loglb"], p["log1m"], p["om"], tm, seq_len)
    o, s_new = hgrn_fn(qs, bc, kk, v, sog)
    uu = jnp.concatenate([buf0, u.reshape(nseq, seq_len, D_MODEL)], axis=1)
    um2 = uu[:, 0:seq_len].reshape(n, D_MODEL)
    um1 = uu[:, 1:seq_len + 1].reshape(n, D_MODEL)
    x = _mixer_out(o, um2, um1, u, bg, sa, sbg, x, p["conv_w"], p["w_a"], p["w_b"], p["w_o"],
                   p["ln_g"][1], p["ln_b"][1], tm)
    x = _ffn_ln(x, p["ffn2_wgu"], p["ffn2_wd"], p["ln_g"][2], p["ln_b"][2], tm)
    return x, s_new, uu[:, seq_len:]


def _layer_long(x, s0, buf0, p, *, seq_len, tm):
    x = _ffn_ln(x, p["ffn1_wgu"], p["ffn1_wd"], p["ln_g"][0], p["ln_b"][0], tm)
    x, s_new, new_buf = _mixer(x, p, s0, buf0, seq_len=seq_len, tm=tm, chunk=LANES)
    x = _ffn_ln(x, p["ffn2_wgu"], p["ffn2_wd"], p["ln_g"][2], p["ln_b"][2], tm)
    return x, s_new, new_buf


def kernel(x_prompt, x_sample, state_hgrn, state_conv, meta_tokens, w_in, hgrn_lb_logits, hgrn_norm_w,
           conv_w, w_branch_a, w_branch_b, w_out, ffn1_w_gu, ffn1_w_down, ffn2_w_gu, ffn2_w_down,
           ln_g, ln_b):
    bp, seq, _ = x_prompt.shape
    bs, dec_seq, _ = x_sample.shape
    loglb, log1m, om = _lower_bounds(hgrn_lb_logits.astype(F32))

    def layer_params(l):
        return dict(
            w_in=w_in[l].astype(BF16), loglb=loglb[l:l + 1], log1m=log1m[l:l + 1], om=om[l:l + 1],
            norm_w=hgrn_norm_w[l][None, :], conv_w=conv_w[l],
            w_a=w_branch_a[l].astype(BF16), w_b=w_branch_b[l].astype(BF16), w_o=w_out[l].astype(BF16),
            ffn1_wgu=ffn1_w_gu[l].astype(BF16), ffn1_wd=ffn1_w_down[l].astype(BF16),
            ffn2_wgu=ffn2_w_gu[l].astype(BF16), ffn2_wd=ffn2_w_down[l].astype(BF16),
            ln_g=[ln_g[l, i][None, :] for i in range(3)], ln_b=[ln_b[l, i][None, :] for i in range(3)])

    xm = meta_tokens.astype(F32)
    xp = x_prompt.reshape(bp * seq, D_MODEL)
    xs = x_sample.reshape(bs * dec_seq, D_MODEL)
    s0_meta = jnp.zeros((1, HEADS, DK, DV), F32)
    hp, cp, cs = [], [], []
    hs_all = None
    for l in range(DEPTH):
        p = layer_params(l)
        xm, s_meta, buf_meta = _layer_short(
            xm, lambda *a: _hgrn(*a, s0_meta, p["norm_w"], seq_len=N_META),
            jnp.zeros((1, CONV_K - 1, D_MODEL), F32), p, seq_len=N_META, tm=N_META)
        xp, s_p, buf_p = _layer_long(xp, s_meta, buf_meta[0], p, seq_len=seq, tm=256)
        xs, hs_all, buf_s = _layer_short(
            xs, lambda *a: _hgrn_step(*a, state_hgrn, hs_all, l, p["norm_w"], seq_len=dec_seq, sb=8),
            state_conv[l], p, seq_len=dec_seq, tm=256)
        hp.append(s_p)
        cp.append(buf_p)
        cs.append(buf_s)
    return (xp.reshape(bp, seq, D_MODEL), xs.reshape(bs, dec_seq, D_MODEL),
            jnp.stack(hp), jnp.stack(cp), hs_all, jnp.stack(cs))
```

```python
import functools
import math

import jax
import jax.numpy as jnp
from jax import lax
from jax.experimental import pallas as pl
from jax.experimental.pallas import tpu as pltpu

D_MODEL = 1024
DEPTH = 4
N_META = 16
HEADS = 8
DK = 128
DV = 128
CONV_K = 3
D_FF = 2816
N_SPLITS = 9
ALPHA = (2 * DEPTH) ** 0.25
LN_EPS = 1e-5
RMS_EPS = 1e-6
LOG2E = math.log2(math.e)

LANES = 128
SUBLANES = 8
HEAD_PAIR = 2 * LANES
VMEM_LIMIT_BYTES = 56 * 1024 * 1024

F32 = jnp.float32
BF16 = jnp.bfloat16

_NT = (((1,), (1,)), ((), ()))
_TN = (((0,), (0,)), ((), ()))


def _const_spec(shape):
    nd = len(shape)
    return pl.BlockSpec(shape, lambda *_: (0,) * nd, pipeline_mode=pl.Buffered(1))


def _layernorm(y, g, b):
    mu = jnp.mean(y, axis=-1, keepdims=True)
    d = y - mu
    var = jnp.mean(d * d, axis=-1, keepdims=True)
    return d * lax.rsqrt(var + LN_EPS) * g + b


def _sigmoid(x):
    return 1.0 / (1.0 + jnp.exp(-x))


def _silu(x):
    return x * _sigmoid(x)


def _lb_kernel(logits_ref, loglb_ref, log1m_ref, om_ref):
    x = logits_ref[...]
    rows = [x[l:l + 1, :] for l in range(DEPTH)]
    m = functools.reduce(jnp.maximum, rows)
    e = [jnp.exp(r - m) for r in rows]
    s = functools.reduce(lambda a, c: a + c, e)
    cs = []
    acc = None
    for l in range(DEPTH):
        p = e[l] / s
        acc = p if acc is None else acc + p
        cs.append(acc)
    for l in range(DEPTH):
        lb = cs[l] - cs[0]
        loglb_ref[l:l + 1, :] = jnp.log(lb)
        log1m_ref[l:l + 1, :] = jnp.log1p(-lb)
        om_ref[l:l + 1, :] = 1.0 - lb


def _lower_bounds(logits):
    shp = jax.ShapeDtypeStruct(logits.shape, F32)
    return pl.pallas_call(_lb_kernel, out_shape=(shp, shp, shp), name="lb_prep")(logits)


def _ffn_ln_kernel(x_ref, wgu_ref, wd_ref, g_ref, b_ref, o_ref, *, parts):
    sub = x_ref.shape[0] // parts
    for i in range(parts):
        rows = slice(i * sub, (i + 1) * sub)
        x = x_ref[rows, :]
        gu = jnp.dot(x.astype(BF16), wgu_ref[...], preferred_element_type=F32)
        h = _silu(gu[:, :D_FF]) * gu[:, D_FF:]
        y = jnp.dot(h.astype(BF16), wd_ref[...], preferred_element_type=F32)
        o_ref[rows, :] = _layernorm(ALPHA * x + 0.5 * y, g_ref[...], b_ref[...])


def _ffn_ln(x, wgu, wd, g, b, tm, parts=1):
    n = x.shape[0]
    row = pl.BlockSpec((tm, D_MODEL), lambda i: (i, 0))
    return pl.pallas_call(
        functools.partial(_ffn_ln_kernel, parts=parts),
        out_shape=jax.ShapeDtypeStruct((n, D_MODEL), F32),
        grid=(n // tm,),
        in_specs=[row, _const_spec(wgu.shape), _const_spec(wd.shape),
                  _const_spec(g.shape), _const_spec(b.shape)],
        out_specs=row,
        compiler_params=pltpu.CompilerParams(
            dimension_semantics=("parallel",), vmem_limit_bytes=VMEM_LIMIT_BYTES),
        name="ffn_ln",
    )(x, wgu, wd, g, b)


def _forget_terms(fp, loglb, log1m, om):
    e = jnp.exp(-jnp.abs(fp))
    one_e = 1.0 + e
    log_sig = jnp.minimum(fp, 0.0) - jnp.log(one_e)
    c = log1m + log_sig
    log2_f = (jnp.maximum(loglb, c) + jnp.log(1.0 + jnp.exp(-jnp.abs(loglb - c)))) * LOG2E
    k = om * (jnp.where(fp >= 0.0, e, 1.0) / one_e)
    return log2_f, k


def _segments(n_rows, chunk):
    sub = min(n_rows, LANES)
    return tuple((s, sub, chunk) for s in range(0, n_rows, sub))


def _chunk_cumsum(log2_f, bc_ref, segments):
    tris = {}
    for start, size, chunk in segments:
        if (size, chunk) not in tris:
            r = lax.broadcasted_iota(jnp.int32, (size, size), 0)
            cidx = lax.broadcasted_iota(jnp.int32, (size, size), 1)
            tris[size, chunk] = jnp.where((cidx <= r) & (r // chunk == cidx // chunk), 1.0, 0.0).astype(BF16)
        g = log2_f[start:start + size, :]
        g_hi = g.astype(BF16)
        g_lo = (g - g_hi.astype(F32)).astype(BF16)
        cum = jnp.dot(tris[size, chunk], jnp.concatenate([g_hi, g_lo], axis=1), preferred_element_type=F32)
        bc_ref[start:start + size, :] = cum[:, :D_MODEL] + cum[:, D_MODEL:]


def _inproj_kernel(x_ref, w_ref, loglb_ref, log1m_ref, om_ref,
                   qs_ref, bc_ref, kk_ref, v_ref, sog_ref, bg_ref, u_ref, sa_ref, sb_ref,
                   *, segments):
    xb = x_ref[...].astype(BF16)

    def z(j):
        return jnp.dot(xb, w_ref[:, j * D_MODEL:(j + 1) * D_MODEL], preferred_element_type=F32)

    qs_ref[...] = _silu(z(0))
    log2_f, k = _forget_terms(z(1), loglb_ref[...], log1m_ref[...], om_ref[...])
    kk_ref[...] = k
    _chunk_cumsum(log2_f, bc_ref, segments)
    v_ref[...] = z(2)
    sog_ref[...] = _silu(z(3))
    bg_ref[...] = z(4)
    u_ref[...] = z(5) * z(6)
    sa_ref[...] = _sigmoid(z(7))
    sb_ref[...] = _sigmoid(z(8))


def _inproj(x, w_in, loglb, log1m, om, segments):
    n = x.shape[0]
    row = pl.BlockSpec((n, D_MODEL), lambda i: (i, 0))
    vec = _const_spec((1, D_MODEL))
    f = jax.ShapeDtypeStruct((n, D_MODEL), F32)
    return pl.pallas_call(
        functools.partial(_inproj_kernel, segments=segments),
        out_shape=(f,) * N_SPLITS,
        grid=(1,),
        in_specs=[row, _const_spec(w_in.shape), vec, vec, vec],
        out_specs=(row,) * N_SPLITS,
        compiler_params=pltpu.CompilerParams(
            dimension_semantics=("parallel",), vmem_limit_bytes=VMEM_LIMIT_BYTES),
        name="inproj",
    )(x, w_in, loglb, log1m, om)


def _blockdiag(a):
    zero = jnp.zeros((a.shape[0], LANES), a.dtype)
    return jnp.concatenate([jnp.concatenate([a[:, :LANES], zero], axis=1),
                            jnp.concatenate([zero, a[:, LANES:]], axis=1)], axis=0)


def _cross_block_scores(q, k, b, chunk):
    row = lax.broadcasted_iota(jnp.int32, (chunk, 2 * chunk), 0)
    col = lax.broadcasted_iota(jnp.int32, (chunk, 2 * chunk), 1) % chunk
    scores = None
    half = chunk // 2
    while half >= SUBLANES:
        nb = chunk // (2 * half)
        q3 = q.reshape(nb, 2 * half, HEAD_PAIR)
        k3 = k.reshape(nb, 2 * half, HEAD_PAIR)
        b3 = b.reshape(nb, 2 * half, HEAD_PAIR)
        bref = b3[:, half - 1:half, :]
        qu = q3[:, half:, :] * jnp.exp2(b3[:, half:, :] - bref)
        kl = k3[:, :half, :] * jnp.exp2(bref - b3[:, :half, :])
        zero = jnp.zeros((nb, half, HEAD_PAIR), BF16)
        qh = jnp.concatenate([zero, qu.astype(BF16)], axis=1).reshape(chunk, HEAD_PAIR)
        kh = jnp.concatenate([kl.astype(BF16), zero], axis=1).reshape(chunk, HEAD_PAIR)
        part = lax.dot_general(qh, _blockdiag(kh), _NT, preferred_element_type=F32)
        if nb > 1:
            part = jnp.where(row // (2 * half) == col // (2 * half), part, 0.0)
        scores = part if scores is None else scores + part
        half //= 2
    return scores


def _hgrn_chunk(tile, row_of, st0, st1, chunk, row0):
    rows = pl.ds(row0, chunk)
    q = tile("q", rows)
    k = tile("k", rows)
    v = tile("v", rows)
    b = tile("b", rows)
    vb = v.astype(BF16)

    bl = b[chunk - 1:chunk, :]
    st_bd = _blockdiag(jnp.concatenate([st0.astype(BF16), st1.astype(BF16)], axis=1))
    o = lax.dot_general((q * jnp.exp2(b)).astype(BF16), st_bd, _NT, preferred_element_type=F32)
    kd = k * jnp.exp2(bl - b)
    upd = lax.dot_general(vb, kd.astype(BF16), _TN, preferred_element_type=F32)
    decay = jnp.exp2(bl)
    st0 = st0 * decay[:, :LANES] + upd[:LANES, :LANES]
    st1 = st1 * decay[:, LANES:] + upd[LANES:, LANES:]

    if chunk >= 2 * SUBLANES:
        scores = _cross_block_scores(q, k, b, chunk)
        o = o + jnp.dot(scores.astype(BF16), _blockdiag(vb), preferred_element_type=F32)

    t_idx = lax.broadcasted_iota(jnp.int32, (SUBLANES, 1), 0)
    blocks = []
    for blk in range(chunk // SUBLANES):
        r = row0 + blk * SUBLANES
        qb = q[blk * SUBLANES:(blk + 1) * SUBLANES, :]
        bb = b[blk * SUBLANES:(blk + 1) * SUBLANES, :]
        acc = None
        for s in range(SUBLANES):
            term = []
            for i in range(2):
                hl = slice(i * LANES, (i + 1) * LANES)
                prod = qb[:, hl] * row_of("k", r + s, i) * jnp.exp2(bb[:, hl] - row_of("b", r + s, i))
                a = jnp.sum(prod, axis=-1, keepdims=True)
                term.append(jnp.where(t_idx >= s, a, 0.0) * row_of("v", r + s, i))
            term = jnp.concatenate(term, axis=1)
            acc = term if acc is None else acc + term
        blocks.append(acc)
    return o + jnp.concatenate(blocks, axis=0), st0, st1


def _rms_gate(o, nw, gate):
    return o * lax.rsqrt(jnp.mean(o * o, axis=-1, keepdims=True) + RMS_EPS) * nw * gate


def _hgrn_kernel(qs_ref, bc_ref, kk_ref, v_ref, sog_ref, s0_ref, nw_ref, o_ref, sn_ref, *, seq_len):
    nw = nw_ref[...]
    refs = dict(q=qs_ref, k=kk_ref, v=v_ref, b=bc_ref)
    for h in range(0, HEADS, 2):
        lanes = slice(h * LANES, (h + 2) * LANES)
        o, st0, st1 = _hgrn_chunk(lambda n, rows: refs[n][0, rows, lanes],
                                  lambda n, r, i: refs[n][0, pl.ds(r, 1), pl.ds((h + i) * LANES, LANES)],
                                  s0_ref[0, h].T, s0_ref[0, h + 1].T, seq_len, 0)
        sn_ref[0, h] = st0.T
        sn_ref[0, h + 1] = st1.T
        for i in range(2):
            hl = slice((h + i) * LANES, (h + i + 1) * LANES)
            o_ref[0, :, hl] = _rms_gate(o[:, i * LANES:(i + 1) * LANES], nw, sog_ref[0, :, hl])


def _hgrn(qs, bc, kk, v, sog, s0, nw, *, seq_len):
    n = qs.shape[0]
    nseq = n // seq_len
    r3 = lambda a: a.reshape(nseq, seq_len, D_MODEL)
    row = pl.BlockSpec((1, seq_len, D_MODEL), lambda i: (i, 0, 0))
    st = pl.BlockSpec((1, HEADS, DK, DV), lambda i: (i, 0, 0, 0))
    o, s_new = pl.pallas_call(
        functools.partial(_hgrn_kernel, seq_len=seq_len),
        out_shape=(jax.ShapeDtypeStruct((nseq, seq_len, D_MODEL), F32),
                   jax.ShapeDtypeStruct((nseq, HEADS, DK, DV), F32)),
        grid=(nseq,),
        in_specs=[row] * 5 + [st, _const_spec((1, DV))],
        out_specs=(row, st),
        compiler_params=pltpu.CompilerParams(
            dimension_semantics=("parallel",), vmem_limit_bytes=VMEM_LIMIT_BYTES),
        name="hgrn",
    )(r3(qs), r3(bc), r3(kk), r3(v), r3(sog), s0, nw)
    return o.reshape(n, D_MODEL), s_new


def _hgrn_step_kernel(qs_ref, bc_ref, kk_ref, v_ref, sog_ref, s0_ref, nw_ref, prev_ref, o_ref, sn_ref,
                      *, sb, seq_len):
    del prev_ref
    nw = nw_ref[...]
    t_idx = lax.broadcasted_iota(jnp.int32, (seq_len, 1), 0)
    n_pieces = 3
    k_rows = -(-(seq_len + n_pieces + 1) // SUBLANES) * SUBLANES
    pad_rows = k_rows - seq_len - n_pieces
    piece_row = lax.broadcasted_iota(jnp.int32, (k_rows - seq_len, HEAD_PAIR), 0) < n_pieces
    right_half = lax.broadcasted_iota(jnp.int32, (k_rows - seq_len, HEAD_PAIR), 1) >= LANES
    ones_rows = jnp.where(piece_row & right_half, 1.0, 0.0)

    def seq_body(s, carry):
        for h in range(HEADS):
            lanes = slice(h * LANES, (h + 1) * LANES)
            q = qs_ref[s, :, lanes]
            k = kk_ref[s, :, lanes]
            v = v_ref[s, :, lanes]
            b = bc_ref[s, :, lanes]
            st = s0_ref[0, s, h]
            bl = b[seq_len - 1:seq_len, :]
            o = jnp.dot((q * jnp.exp2(b)).astype(BF16), st.astype(BF16), preferred_element_type=F32)
            kd = k * jnp.exp2(bl - b)
            d_hi = jnp.exp2(bl).astype(BF16).astype(F32)
            rest_hi = jnp.exp2(bl) - d_hi
            d_mid = rest_hi.astype(BF16).astype(F32)
            d_lo = rest_hi - d_mid
            lhs = jnp.concatenate([kd, d_hi, d_mid, d_lo, jnp.zeros((pad_rows, LANES), F32)], axis=0)
            rhs = jnp.concatenate([jnp.concatenate([v, jnp.zeros((seq_len, LANES), F32)], axis=1), ones_rows],
                                  axis=0)
            res = lax.dot_general(lhs.astype(BF16), rhs.astype(BF16), _TN, preferred_element_type=F32)
            sn_ref[0, s, h] = st * res[:, LANES:] + res[:, :LANES]
            for r in range(seq_len):
                a = jnp.sum(q * k[r:r + 1, :] * jnp.exp2(b - b[r:r + 1, :]), axis=-1, keepdims=True)
                o = o + jnp.where(t_idx >= r, a, 0.0) * v[r:r + 1, :]
            o_ref[s, :, lanes] = _rms_gate(o, nw, sog_ref[s, :, lanes])
        return carry

    lax.fori_loop(0, sb, seq_body, 0)


def _hgrn_step(qs, bc, kk, v, sog, s_all, s_new_all, layer, nw, *, seq_len, sb):
    n = qs.shape[0]
    nseq = n // seq_len
    r3 = lambda a: a.reshape(nseq, seq_len, D_MODEL)
    row = pl.BlockSpec((sb, seq_len, D_MODEL), lambda i: (i, 0, 0))
    st = pl.BlockSpec((1, sb, HEADS, DK, DV), lambda i: (layer, i, 0, 0, 0))
    operands = [r3(qs), r3(bc), r3(kk), r3(v), r3(sog), s_all, nw, s_new_all]
    in_specs = [row] * 5 + [st, _const_spec((1, DV)), pl.BlockSpec(memory_space=pl.ANY)]
    aliases = {len(operands) - 1: 1}
    o, s_new_all = pl.pallas_call(
        functools.partial(_hgrn_step_kernel, sb=sb, seq_len=seq_len),
        out_shape=(jax.ShapeDtypeStruct((nseq, seq_len, D_MODEL), F32),
                   jax.ShapeDtypeStruct(s_all.shape, F32)),
        grid=(nseq // sb,),
        in_specs=in_specs,
        out_specs=(row, st),
        input_output_aliases=aliases,
        compiler_params=pltpu.CompilerParams(
            dimension_semantics=("parallel",), vmem_limit_bytes=VMEM_LIMIT_BYTES),
        name="hgrn_step",
    )(*operands)
    return o.reshape(n, D_MODEL), s_new_all


def _mixer_tail(o, conv, bg, sa, sb, x, wa_ref, wb_ref, wo_ref, g_ref, b_ref):
    y_b = jnp.dot((bg * conv).astype(BF16), wb_ref[...], preferred_element_type=F32)
    y_a = jnp.dot(o.astype(BF16), wa_ref[...], preferred_element_type=F32)
    m = sa * y_a + sb * y_b
    out = jnp.dot(m.astype(BF16), wo_ref[...], preferred_element_type=F32)
    return _layernorm(ALPHA * x + out, g_ref[...], b_ref[...])


def _mixer_out_kernel(o_ref, um2_ref, um1_ref, u_ref, bg_ref, sa_ref, sb_ref, x_ref,
                      cw_ref, wa_ref, wb_ref, wo_ref, g_ref, b_ref, out_ref):
    conv = cw_ref[0:1, :] * um2_ref[...] + cw_ref[1:2, :] * um1_ref[...] + cw_ref[2:3, :] * u_ref[...]
    out_ref[...] = _mixer_tail(o_ref[...], conv, bg_ref[...], sa_ref[...], sb_ref[...], x_ref[...],
                               wa_ref, wb_ref, wo_ref, g_ref, b_ref)


def _mixer_out(o, um2, um1, u, bg, sa, sb, x, cw, wa, wb, wo, g, b, tm):
    n = x.shape[0]
    row = pl.BlockSpec((tm, D_MODEL), lambda i: (i, 0))
    wspec = _const_spec((D_MODEL, D_MODEL))
    vec = _const_spec((1, D_MODEL))
    return pl.pallas_call(
        _mixer_out_kernel,
        out_shape=jax.ShapeDtypeStruct((n, D_MODEL), F32),
        grid=(n // tm,),
        in_specs=[row] * 8 + [_const_spec((CONV_K, D_MODEL)), wspec, wspec, wspec, vec, vec],
        out_specs=row,
        compiler_params=pltpu.CompilerParams(
            dimension_semantics=("parallel",), vmem_limit_bytes=VMEM_LIMIT_BYTES),
        name="mixer_out",
    )(o, um2, um1, u, bg, sa, sb, x, cw, wa, wb, wo, g, b)


def _mixer_kernel(x_ref, w_ref, loglb_ref, log1m_ref, om_ref, nw_ref, s0_ref, buf0_ref, cw_ref,
                  wa_ref, wb_ref, wo_ref, g_ref, b_ref,
                  out_ref, sn_ref, nbuf_ref,
                  q_s, b_s, k_s, v_s, o_s, st_s, tail_s, *, tm, chunk):
    j = pl.program_id(1)

    @pl.when(j == 0)
    def _():
        for h in range(HEADS):
            st_s[h] = s0_ref[0, h].T
        tail_s[SUBLANES - (CONV_K - 1):SUBLANES, :] = buf0_ref[...]

    x = x_ref[...]
    xb = x.astype(BF16)

    def z(i):
        return jnp.dot(xb, w_ref[:, i * D_MODEL:(i + 1) * D_MODEL], preferred_element_type=F32)

    q_s[...] = _silu(z(0))
    log2_f, k = _forget_terms(z(1), loglb_ref[...], log1m_ref[...], om_ref[...])
    k_s[...] = k
    _chunk_cumsum(log2_f, b_s, _segments(tm, chunk))
    v_s[...] = z(2)
    sog = _silu(z(3))
    bg = z(4)
    u = z(5) * z(6)
    sa = _sigmoid(z(7))
    sb = _sigmoid(z(8))

    p2 = tail_s[SUBLANES - 2:SUBLANES - 1, :]
    p1 = tail_s[SUBLANES - 1:SUBLANES, :]
    row = lax.broadcasted_iota(jnp.int32, (tm, 1), 0)
    um1 = jnp.where(row == 0, p1, pltpu.roll(u, 1, 0))
    um2 = jnp.where(row == 0, p2, jnp.where(row == 1, p1, pltpu.roll(u, 2, 0)))
    conv = cw_ref[0:1, :] * um2 + cw_ref[1:2, :] * um1 + cw_ref[2:3, :] * u
    tail_s[...] = u[tm - SUBLANES:tm, :]
    nbuf_ref[0] = u[tm - (CONV_K - 1):tm, :]
    y_b = jnp.dot((bg * conv).astype(BF16), wb_ref[...], preferred_element_type=F32)

    nw = nw_ref[...]
    refs = dict(q=q_s, k=k_s, v=v_s, b=b_s)
    for h in range(0, HEADS, 2):
        lanes = slice(h * LANES, (h + 2) * LANES)
        st0 = st_s[h]
        st1 = st_s[h + 1]
        for c in range(tm // chunk):
            rows = slice(c * chunk, (c + 1) * chunk)
            o, st0, st1 = _hgrn_chunk(lambda n, rows: refs[n][rows, lanes],
                                      lambda n, r, i: refs[n][pl.ds(r, 1), pl.ds((h + i) * LANES, LANES)],
                                      st0, st1, chunk, c * chunk)
            for i in range(2):
                hl = slice((h + i) * LANES, (h + i + 1) * LANES)
                o_s[rows, hl] = _rms_gate(o[:, i * LANES:(i + 1) * LANES], nw, sog[rows, hl]).astype(BF16)
        st_s[h] = st0
        st_s[h + 1] = st1

    y_a = jnp.dot(o_s[...], wa_ref[...], preferred_element_type=F32)
    m = sa * y_a + sb * y_b
    out = jnp.dot(m.astype(BF16), wo_ref[...], preferred_element_type=F32)
    out_ref[...] = _layernorm(ALPHA * x + out, g_ref[...], b_ref[...])

    @pl.when(j == pl.num_programs(1) - 1)
    def _():
        for h in range(HEADS):
            sn_ref[0, h] = st_s[h].T


def _mixer(x, p, s0, buf0, *, seq_len, tm, chunk):
    n = x.shape[0]
    nseq = n // seq_len
    tiles = seq_len // tm
    row = pl.BlockSpec((tm, D_MODEL), lambda i, j: (i * tiles + j, 0))
    vec = _const_spec((1, D_MODEL))
    wspec = _const_spec((D_MODEL, D_MODEL))
    rows_f32 = pltpu.VMEM((tm, D_MODEL), F32)
    return pl.pallas_call(
        functools.partial(_mixer_kernel, tm=tm, chunk=chunk),
        out_shape=(jax.ShapeDtypeStruct((n, D_MODEL), F32),
                   jax.ShapeDtypeStruct((nseq, HEADS, DK, DV), F32),
                   jax.ShapeDtypeStruct((nseq, CONV_K - 1, D_MODEL), F32)),
        grid=(nseq, tiles),
        in_specs=[row, _const_spec(p["w_in"].shape), vec, vec, vec, _const_spec((1, DV)),
                  _const_spec((1, HEADS, DK, DV)), _const_spec((CONV_K - 1, D_MODEL)),
                  _const_spec((CONV_K, D_MODEL)), wspec, wspec, wspec, vec, vec],
        out_specs=(row, pl.BlockSpec((1, HEADS, DK, DV), lambda i, j: (i, 0, 0, 0)),
                   pl.BlockSpec((1, CONV_K - 1, D_MODEL), lambda i, j: (i, 0, 0))),
        scratch_shapes=[rows_f32, rows_f32, rows_f32, rows_f32, pltpu.VMEM((tm, D_MODEL), BF16),
                        pltpu.VMEM((HEADS, DV, DK), F32), pltpu.VMEM((SUBLANES, D_MODEL), F32)],
        compiler_params=pltpu.CompilerParams(
            dimension_semantics=("parallel", "arbitrary"), vmem_limit_bytes=VMEM_LIMIT_BYTES),
        name="mixer",
    )(x, p["w_in"], p["loglb"], p["log1m"], p["om"], p["norm_w"], s0, buf0, p["conv_w"],
      p["w_a"], p["w_b"], p["w_o"], p["ln_g"][1], p["ln_b"][1])


def _shifted_conv_inputs(u, buf, seq_len):
    nseq = u.shape[0] // seq_len
    uu = jnp.concatenate([buf, u.reshape(nseq, seq_len, D_MODEL)], axis=1)
    return (uu[:, 0:seq_len].reshape(-1, D_MODEL), uu[:, 1:seq_len + 1].reshape(-1, D_MODEL),
            uu[:, seq_len:])


def _layer_small(x, p, s_all, hs_all, layer, buf_sample, *, n_sample, dec_seq):
    n = x.shape[0]
    segments = _segments(n_sample, dec_seq) + ((n_sample, N_META, N_META),)
    x = _ffn_ln(x, p["ffn1_wgu"], p["ffn1_wd"], p["ln_g"][0], p["ln_b"][0], n)
    qs, bc, kk, v, sog, bg, u, sa, sbg = _inproj(x, p["w_in"], p["loglb"], p["log1m"], p["om"], segments)
    o_s, hs_all = _hgrn_step(*(a[:n_sample] for a in (qs, bc, kk, v, sog)), s_all, hs_all, layer,
                             p["norm_w"], seq_len=dec_seq, sb=8)
    o_m, s_meta = _hgrn(*(a[n_sample:] for a in (qs, bc, kk, v, sog)), jnp.zeros((1, HEADS, DK, DV), F32),
                        p["norm_w"], seq_len=N_META)
    um2_s, um1_s, nbuf_s = _shifted_conv_inputs(u[:n_sample], buf_sample, dec_seq)
    um2_m, um1_m, nbuf_m = _shifted_conv_inputs(u[n_sample:], jnp.zeros((1, CONV_K - 1, D_MODEL), F32), N_META)
    x = _mixer_out(jnp.concatenate([o_s, o_m]), jnp.concatenate([um2_s, um2_m]),
                   jnp.concatenate([um1_s, um1_m]), u, bg, sa, sbg, x, p["conv_w"], p["w_a"], p["w_b"],
                   p["w_o"], p["ln_g"][1], p["ln_b"][1], n)
    x = _ffn_ln(x, p["ffn2_wgu"], p["ffn2_wd"], p["ln_g"][2], p["ln_b"][2], n)
    return x, hs_all, nbuf_s, s_meta, nbuf_m[0]


def _layer_long(x, s0, buf0, p, *, seq_len, tm):
    x = _ffn_ln(x, p["ffn1_wgu"], p["ffn1_wd"], p["ln_g"][0], p["ln_b"][0], 2 * tm, parts=2)
    x, s_new, new_buf = _mixer(x, p, s0, buf0, seq_len=seq_len, tm=tm, chunk=LANES)
    x = _ffn_ln(x, p["ffn2_wgu"], p["ffn2_wd"], p["ln_g"][2], p["ln_b"][2], 2 * tm, parts=2)
    return x, s_new, new_buf


def kernel(x_prompt, x_sample, state_hgrn, state_conv, meta_tokens, w_in, hgrn_lb_logits, hgrn_norm_w,
           conv_w, w_branch_a, w_branch_b, w_out, ffn1_w_gu, ffn1_w_down, ffn2_w_gu, ffn2_w_down,
           ln_g, ln_b):
    bp, seq, _ = x_prompt.shape
    bs, dec_seq, _ = x_sample.shape
    loglb, log1m, om = _lower_bounds(hgrn_lb_logits.astype(F32))

    def layer_params(l):
        return dict(
            w_in=w_in[l].astype(BF16), loglb=loglb[l:l + 1], log1m=log1m[l:l + 1], om=om[l:l + 1],
            norm_w=hgrn_norm_w[l][None, :], conv_w=conv_w[l],
            w_a=w_branch_a[l].astype(BF16), w_b=w_branch_b[l].astype(BF16), w_o=w_out[l].astype(BF16),
            ffn1_wgu=ffn1_w_gu[l].astype(BF16), ffn1_wd=ffn1_w_down[l].astype(BF16),
            ffn2_wgu=ffn2_w_gu[l].astype(BF16), ffn2_wd=ffn2_w_down[l].astype(BF16),
            ln_g=[ln_g[l, i][None, :] for i in range(3)], ln_b=[ln_b[l, i][None, :] for i in range(3)])

    xp = x_prompt.reshape(bp * seq, D_MODEL)
    n_sample = bs * dec_seq
    xsm = jnp.concatenate([x_sample.reshape(n_sample, D_MODEL), meta_tokens.astype(F32)])
    hs_all = jnp.zeros(state_hgrn.shape, F32)
    hp, cp, cs = [], [], []
    for l in range(DEPTH):
        p = layer_params(l)
        xsm, hs_all, buf_s, s_meta, buf_meta = _layer_small(
            xsm, p, state_hgrn, hs_all, l, state_conv[l], n_sample=n_sample, dec_seq=dec_seq)
        xp, s_p, buf_p = _layer_long(xp, s_meta, buf_meta, p, seq_len=seq, tm=256)
        hp.append(s_p)
        cp.append(buf_p)
        cs.append(buf_s)
    xs = xsm[:n_sample]
    return (xp.reshape(bp, seq, D_MODEL), xs.reshape(bs, dec_seq, D_MODEL),
            jnp.stack(hp), jnp.stack(cp), hs_all, jnp.stack(cs))
```
